```python
import math
import functools
import jax
import jax.numpy as jnp
from jax import lax
import numpy as np

D_MODEL = 1024
BATCH = 4
SEQ = 8192
DEPTH = 2
DEC_BATCH = 32
DEC_SEQ = 1
PAST_LEN = 16384
PAGE_SIZE = 128

N_HEADS = 8
KV_HEADS = 4
GQA_GROUP = N_HEADS // KV_HEADS
HEAD_DIM = D_MODEL // (2 * N_HEADS)
V_DIM = 2 * HEAD_DIM
ATTN_OUT = N_HEADS * V_DIM
Q_BLOCK = 128
CHUNK = 128
D_A = D_MODEL
A_GROUPS = 8
A_GROUP_DIM = D_A // A_GROUPS
N_EXPERTS = 32
TOP_K = 4
D_FF = D_MODEL
SWIGLU_ALPHA = 1.702
SWIGLU_LIMIT = 7.0
MOE_BLOCK = 128
Q_W = N_HEADS * 2 * HEAD_DIM
K_W = KV_HEADS * 2 * HEAD_DIM
V_W = KV_HEADS * V_DIM
IN_SIZES = (2 * D_A, Q_W, K_W, V_W, 2 * D_MODEL)
N_IN = sum(IN_SIZES)
IN_SPLITS = [int(o) for o in np.cumsum(IN_SIZES)[:-1]]

kernel_name = 'hybrid_gmlp_diffattn_moe_step'

F32 = jnp.float32


def rms_norm(x, g, eps=1e-6):
    xf = x.astype(F32)
    y = xf * lax.rsqrt(jnp.mean(xf * xf, axis=-1, keepdims=True) + eps)
    return y.astype(x.dtype) * g


def layer_norm(x, g, b, eps=1e-5):
    xf = x.astype(F32)
    mu = jnp.mean(xf, axis=-1, keepdims=True)
    var = jnp.mean(jnp.square(xf - mu), axis=-1, keepdims=True)
    return ((xf - mu) * lax.rsqrt(var + eps)).astype(x.dtype) * g + b


def lambda_init(layer):
    return 0.8 - 0.6 * math.exp(-0.3 * layer)


def chunk_mix_prompt(va, w_s, b_s):
    b, s, _ = va.shape
    tri = jnp.tril(jnp.ones((CHUNK, CHUNK), w_s.dtype))
    vr = va.reshape(b, s // CHUNK, CHUNK, A_GROUPS, A_GROUP_DIM)
    out = jnp.einsum('gij,bnjgc->bnigc', w_s * tri, vr) + b_s.T[None, None, :, :, None]
    return out.reshape(b, s, D_A)


def chunk_mix_sample(va, w_s, b_s):
    b, s, _ = va.shape
    tri = jnp.tril(jnp.ones((s, s), w_s.dtype))
    vr = va.reshape(b, s, A_GROUPS, A_GROUP_DIM)
    out = jnp.einsum('gij,bjgc->bigc', w_s[:, :s, :s] * tri, vr) + b_s[:, :s].T[None, :, :, None]
    return out.reshape(b, s, D_A)


def diff_attn_prompt(q, k, v, lam):
    b, s = q.shape[:2]
    n_qb = s // Q_BLOCK
    qb = q.reshape(b, n_qb, Q_BLOCK, KV_HEADS, GQA_GROUP, 2, HEAD_DIM).transpose(1, 0, 2, 3, 4, 5, 6)
    k_pos = jnp.arange(s)
    scale = HEAD_DIM ** -0.5
    neg = jnp.finfo(F32).min

    def one_block(args):
        qi, i = args
        sc = jnp.einsum('bqhgmd,bkhmd->bhgmqk', qi, k).astype(F32) * scale
        q_pos = i * Q_BLOCK + jnp.arange(Q_BLOCK)
        sc = jnp.where(k_pos[None, :] <= q_pos[:, None], sc, neg)
        p = jax.nn.softmax(sc, axis=-1)
        a = (p[:, :, :, 0] - lam * p[:, :, :, 1]).astype(v.dtype)
        return jnp.einsum('bhgqk,bkhe->bqhge', a, v)

    o = lax.map(one_block, (qb, jnp.arange(n_qb)))
    return o.transpose(1, 0, 2, 3, 4, 5).reshape(b, s, N_HEADS, V_DIM)


def diff_attn_sample(q, k, v, lam, cache_k_l, cache_v_l, page_table):
    db, ds = q.shape[:2]
    k_past = cache_k_l[page_table].reshape(db, -1, KV_HEADS, 2, HEAD_DIM)
    v_past = cache_v_l[page_table].reshape(db, -1, KV_HEADS, V_DIM)
    n_past = k_past.shape[1]
    scale = HEAD_DIM ** -0.5
    neg = jnp.finfo(F32).min
    qg = q.reshape(db, ds, KV_HEADS, GQA_GROUP, 2, HEAD_DIM)
    s_past = jnp.einsum('bqhgmd,bkhmd->bhgmqk', qg, k_past).astype(F32) * scale
    s_new = jnp.einsum('bqhgmd,bkhmd->bhgmqk', qg, k).astype(F32) * scale
    s_new = jnp.where(jnp.tril(jnp.ones((ds, ds), bool)), s_new, neg)
    p = jax.nn.softmax(jnp.concatenate([s_past, s_new], axis=-1), axis=-1)
    a = (p[:, :, :, 0] - lam * p[:, :, :, 1]).astype(v.dtype)
    o = (jnp.einsum('bhgqk,bkhe->bqhge', a[..., :n_past], v_past)
         + jnp.einsum('bhgqk,bkhe->bqhge', a[..., n_past:], v))
    return o.reshape(db, ds, N_HEADS, V_DIM)


def clamped_swiglu(z):
    x_glu = jnp.minimum(z[..., ::2], SWIGLU_LIMIT)
    x_lin = jnp.clip(z[..., 1::2], -SWIGLU_LIMIT, SWIGLU_LIMIT)
    return x_glu * jax.nn.sigmoid(SWIGLU_ALPHA * x_glu) * (x_lin + 1)


def moe(h, router_w, router_b, w1, b1, w2, b2):
    t = h.shape[0]
    logits = (h @ router_w + router_b).astype(F32)
    top_val, top_idx = lax.top_k(logits, TOP_K)
    gates = jax.nn.softmax(top_val, axis=-1).astype(h.dtype)
    n_assign = t * TOP_K
    e_flat = top_idx.reshape(n_assign)
    tok_flat = jnp.repeat(jnp.arange(t, dtype=jnp.int32), TOP_K)
    g_flat = gates.reshape(n_assign)
    order = jnp.argsort(e_flat)
    e_sorted = e_flat[order]
    counts = jnp.bincount(e_flat, length=N_EXPERTS)
    starts = jnp.cumsum(counts) - counts
    padded = (counts + MOE_BLOCK - 1) // MOE_BLOCK * MOE_BLOCK
    pad_ends = jnp.cumsum(padded)
    pad_starts = pad_ends - padded
    dest = pad_starts[e_sorted] + jnp.arange(n_assign) - starts[e_sorted]
    n_blocks = -(-n_assign // MOE_BLOCK) + N_EXPERTS
    n_slots = n_blocks * MOE_BLOCK
    slot_tok = jnp.full((n_slots,), t, jnp.int32).at[dest].set(tok_flat[order])
    slot_gate = jnp.zeros((n_slots,), h.dtype).at[dest].set(g_flat[order])
    block_expert = jnp.minimum(
        jnp.searchsorted(pad_ends, jnp.arange(n_blocks) * MOE_BLOCK, side='right'), N_EXPERTS - 1)
    h_pad = jnp.concatenate([h, jnp.zeros((1, h.shape[1]), h.dtype)], axis=0)
    xb = h_pad[slot_tok].reshape(n_blocks, MOE_BLOCK, h.shape[1])

    def expert_block(args):
        xe, e = args
        z = xe @ w1[e] + b1[e]
        return clamped_swiglu(z) @ w2[e] + b2[e]

    yb = lax.map(expert_block, (xb, block_expert)).reshape(n_slots, h.shape[1])
    y = jnp.zeros_like(h_pad).at[slot_tok].add(yb * slot_gate[:, None])
    return y[:t]


def layer_forward(x, c, lp, lam, lam_init, attend, chunk_mix):
    mods = jnp.split(jax.nn.silu(c) @ lp['ada_w'] + lp['ada_b'], 6, axis=-1)
    sh1, sc1, g1, sh2, sc2, g2 = [m[:, None, :] for m in mods]
    h = rms_norm(x, lp['norm1_g']) * (1 + sc1) + sh1
    z = h @ lp['w_in']
    uv, q, k, v, gl = jnp.split(z, IN_SPLITS, axis=-1)
    u, va = jnp.split(jax.nn.gelu(uv, approximate=False), 2, axis=-1)
    va = layer_norm(va, lp['ln_v_g'], lp['ln_v_b'])
    bsz, s = x.shape[:2]
    q = rms_norm(q.reshape(bsz, s, N_HEADS, 2, HEAD_DIM), lp['q_norm_g'])
    k = rms_norm(k.reshape(bsz, s, KV_HEADS, 2, HEAD_DIM), lp['k_norm_g'])
    v = v.reshape(bsz, s, KV_HEADS, V_DIM)
    o = attend(q, k, v, lam)
    o = rms_norm(o, lp['subln_g']) * (1 - lam_init)
    mixed = u * chunk_mix(va, lp['w_spatial'], lp['b_spatial'])
    g_a, g_b = jnp.split(jax.nn.sigmoid(gl), 2, axis=-1)
    merged = (g_a * (mixed @ lp['w_branch_a'])
              + g_b * (o.reshape(bsz, s, ATTN_OUT) @ lp['w_branch_b']))
    x = x + g1 * (merged @ lp['w_out'])
    h2 = rms_norm(x, lp['norm2_g']) * (1 + sc2) + sh2
    y = moe(h2.reshape(bsz * s, D_MODEL), lp['router_w'], lp['router_b'],
            lp['w1'], lp['b1'], lp['w2'], lp['b2']).reshape(x.shape)
    x = x + g2 * y
    return x, k, v, va


def setup_inputs(seed: int = 0) -> dict:
    key = jax.random.key(seed)
    keys = list(jax.random.split(key, 48))

    def nrm(shape, scale):
        return jax.random.normal(keys.pop(), shape, F32) * scale

    n_pages = PAST_LEN // PAGE_SIZE
    n_used = DEC_BATCH * n_pages
    n_phys = n_used + n_used // 4
    page_table = jax.random.permutation(keys.pop(), n_phys)[:n_used].reshape(DEC_BATCH, n_pages).astype(jnp.int32)
    d = D_MODEL
    return {
        'x_prompt': nrm((BATCH, SEQ, d), 1.0),
        'x_sample': nrm((DEC_BATCH, DEC_SEQ, d), 1.0),
        'c_prompt': nrm((BATCH, d), 1.0),
        'c_sample': nrm((DEC_BATCH, d), 1.0),
        'cache_k': nrm((DEPTH, n_phys, PAGE_SIZE, KV_HEADS, 2, HEAD_DIM), 1.0),
        'cache_v': nrm((DEPTH, n_phys, PAGE_SIZE, KV_HEADS, V_DIM), 1.0),
        'page_table': page_table,
        'ada_w': nrm((DEPTH, d, 6 * d), 0.2 * d ** -0.5),
        'ada_b': nrm((DEPTH, 6 * d), 0.02),
        'norm1_g': 1.0 + nrm((DEPTH, d), 0.02),
        'norm2_g': 1.0 + nrm((DEPTH, d), 0.02),
        'w_in': nrm((DEPTH, d, N_IN), d ** -0.5),
        'q_norm_g': 1.0 + nrm((DEPTH, HEAD_DIM), 0.02),
        'k_norm_g': 1.0 + nrm((DEPTH, HEAD_DIM), 0.02),
        'lambda_q1': nrm((DEPTH, HEAD_DIM), 0.1),
        'lambda_k1': nrm((DEPTH, HEAD_DIM), 0.1),
        'lambda_q2': nrm((DEPTH, HEAD_DIM), 0.1),
        'lambda_k2': nrm((DEPTH, HEAD_DIM), 0.1),
        'subln_g': 1.0 + nrm((DEPTH, V_DIM), 0.02),
        'ln_v_g': 1.0 + nrm((DEPTH, D_A), 0.02),
        'ln_v_b': nrm((DEPTH, D_A), 0.02),
        'w_spatial': nrm((DEPTH, A_GROUPS, CHUNK, CHUNK), 0.5 * CHUNK ** -0.5),
        'b_spatial': 1.0 + nrm((DEPTH, A_GROUPS, CHUNK), 0.1),
        'w_branch_a': nrm((DEPTH, D_A, d), D_A ** -0.5),
        'w_branch_b': nrm((DEPTH, ATTN_OUT, d), ATTN_OUT ** -0.5),
        'w_out': nrm((DEPTH, d, d), d ** -0.5),
        'router_w': nrm((DEPTH, d, N_EXPERTS), d ** -0.5),
        'router_b': nrm((DEPTH, N_EXPERTS), 0.01),
        'w1': nrm((DEPTH, N_EXPERTS, d, 2 * D_FF), d ** -0.5),
        'b1': nrm((DEPTH, N_EXPERTS, 2 * D_FF), 0.02),
        'w2': nrm((DEPTH, N_EXPERTS, D_FF, d), D_FF ** -0.5),
        'b2': nrm((DEPTH, N_EXPERTS, d), 0.02),
    }


def reference(x_prompt, x_sample, c_prompt, c_sample, cache_k, cache_v, page_table,
              ada_w, ada_b, norm1_g, norm2_g, w_in, q_norm_g, k_norm_g,
              lambda_q1, lambda_k1, lambda_q2, lambda_k2, subln_g, ln_v_g, ln_v_b,
              w_spatial, b_spatial, w_branch_a, w_branch_b, w_out,
              router_w, router_b, w1, b1, w2, b2):
    yp, ys = x_prompt, x_sample
    kp_list, vp_list, ks_list, vs_list, cv_list = [], [], [], [], []
    for l in range(DEPTH):
        lp = {
            'ada_w': ada_w[l], 'ada_b': ada_b[l], 'norm1_g': norm1_g[l], 'norm2_g': norm2_g[l],
            'w_in': w_in[l], 'q_norm_g': q_norm_g[l], 'k_norm_g': k_norm_g[l], 'subln_g': subln_g[l],
            'ln_v_g': ln_v_g[l], 'ln_v_b': ln_v_b[l], 'w_spatial': w_spatial[l], 'b_spatial': b_spatial[l],
            'w_branch_a': w_branch_a[l], 'w_branch_b': w_branch_b[l], 'w_out': w_out[l],
            'router_w': router_w[l], 'router_b': router_b[l],
            'w1': w1[l], 'b1': b1[l], 'w2': w2[l], 'b2': b2[l],
        }
        lam_init = lambda_init(l)
        lam = (jnp.exp(jnp.sum(lambda_q1[l].astype(F32) * lambda_k1[l].astype(F32)))
               - jnp.exp(jnp.sum(lambda_q2[l].astype(F32) * lambda_k2[l].astype(F32)))
               + lam_init)
        yp, kp, vp, _ = layer_forward(yp, c_prompt, lp, lam, lam_init,
                                      diff_attn_prompt, chunk_mix_prompt)
        attend_s = functools.partial(diff_attn_sample, cache_k_l=cache_k[l],
                                     cache_v_l=cache_v[l], page_table=page_table)
        ys, ks, vs, cvs = layer_forward(ys, c_sample, lp, lam, lam_init,
                                        attend_s, chunk_mix_sample)
        kp_list.append(kp)
        vp_list.append(vp)
        ks_list.append(ks)
        vs_list.append(vs)
        cv_list.append(cvs)
    k_prompt = jnp.stack(kp_list)
    v_prompt = jnp.stack(vp_list)
    k_sample = jnp.stack(ks_list)
    v_sample = jnp.stack(vs_list)
    chunk_v_sample = jnp.stack(cv_list)
    return (yp, ys, k_prompt, v_prompt, k_sample, v_sample, chunk_v_sample)
```

```python
import functools
import math

import jax
import jax.numpy as jnp
from jax import lax
from jax.experimental import pallas as pl
from jax.experimental.pallas import tpu as pltpu

F32 = jnp.float32
BF16 = jnp.bfloat16

LANES = 128
HEAD_DIM = 64
N_HEADS = 8
KV_HEADS = 4
GQA_GROUP = N_HEADS // KV_HEADS
V_DIM = 2 * HEAD_DIM
CHUNK = 128
A_GROUPS = 8
N_EXPERTS = 32
TOP_K = 4
SWIGLU_ALPHA = 1.702
SWIGLU_LIMIT = 7.0
NEG = -1e30

TM_DENSE = 256
TQ = 256
PAGES_PER_STEP = 8
TB_PROMPT = 256
TB_SAMPLE = 16
VMEM_LIMIT = 48 * 1024 * 1024


def _const_spec(shape):
    nd = len(shape)
    return pl.BlockSpec(shape, lambda *_: (0,) * nd, pipeline_mode=pl.Buffered(1))


def _gelu(x):
    return x * (lax.erf(x * (1.0 / math.sqrt(2.0))) + 1.0) * 0.5


def _rms(x, eps=1e-6):
    return x * lax.rsqrt(jnp.mean(x * x, axis=-1, keepdims=True) + eps)


def _nt_dot(a, b):
    return lax.dot_general(a, b, (((1,), (1,)), ((), ())), preferred_element_type=F32)


def _ada_kernel(c_ref, w_ref, b_ref, o_ref):
    c = c_ref[...]
    sc = (c * jax.nn.sigmoid(c)).astype(BF16)
    o_ref[0] = jnp.dot(sc, w_ref[0].astype(BF16), preferred_element_type=F32) + b_ref[0]


def _ada_mods(c_all, ada_w, ada_b):
    depth, d, n = ada_w.shape
    r = c_all.shape[0]
    tn = 512
    return pl.pallas_call(
        _ada_kernel,
        grid=(depth, n // tn),
        in_specs=[pl.BlockSpec((r, d), lambda l, j: (0, 0)),
                  pl.BlockSpec((1, d, tn), lambda l, j: (l, 0, j)),
                  pl.BlockSpec((1, 1, tn), lambda l, j: (l, 0, j))],
        out_specs=pl.BlockSpec((1, r, tn), lambda l, j: (l, 0, j)),
        out_shape=jax.ShapeDtypeStruct((depth, r, n), F32),
        compiler_params=pltpu.CompilerParams(dimension_semantics=("parallel", "parallel")),
        name="ada",
    )(c_all, ada_w, ada_b.reshape(depth, 1, n))


def _half_rms(t, g_row, first):
    s = t * t
    s1 = jnp.sum(jnp.where(first, s, 0.0), axis=-1, keepdims=True)
    s2 = jnp.sum(jnp.where(first, 0.0, s), axis=-1, keepdims=True)
    r = jnp.where(first, lax.rsqrt(s1 * (1.0 / HEAD_DIM) + 1e-6),
                  lax.rsqrt(s2 * (1.0 / HEAD_DIM) + 1e-6))
    return t * r * g_row


def _inproj_kernel(x_ref, sc_ref, sh_ref, n1g_ref, w_ref, lng_ref, lnb_ref, qg_ref, kg_ref,
                   ws_ref, bs_ref, mixed_ref, q_ref, k_ref, v_ref, kb_ref, vb_ref, ga_ref, gb_ref,
                   *cv_ref, sample):
    d = x_ref.shape[-1]
    tm = x_ref.shape[1]
    x = x_ref[0]
    h = (_rms(x) * n1g_ref[...] * (1.0 + sc_ref[0]) + sh_ref[0]).astype(BF16)

    def proj(lo, width):
        return jnp.dot(h, w_ref[:, lo:lo + width], preferred_element_type=F32)

    u = _gelu(proj(0, d))
    va = _gelu(proj(d, d))
    dv = va - jnp.mean(va, axis=-1, keepdims=True)
    van = dv * lax.rsqrt(jnp.mean(dv * dv, axis=-1, keepdims=True) + 1e-5) * lng_ref[...] + lnb_ref[...]
    if sample:
        cv_ref[0][0] = van
        mixed_ref[0] = (u * (van * ws_ref[...] + bs_ref[...])).astype(BF16)
    else:
        vb16 = van.astype(BF16)
        for c in range(tm // CHUNK):
            rs = slice(c * CHUNK, (c + 1) * CHUNK)
            for g in range(A_GROUPS):
                cs = slice(g * LANES, (g + 1) * LANES)
                mix = jnp.dot(ws_ref[g], vb16[rs, cs], preferred_element_type=F32) + bs_ref[:, cs]
                mixed_ref[0, rs, cs] = (u[rs, cs] * mix).astype(BF16)

    first = lax.broadcasted_iota(jnp.int32, (1, LANES), 1) < HEAD_DIM
    zq = proj(2 * d, N_HEADS * LANES)
    for hh in range(N_HEADS):
        cs = slice(hh * LANES, (hh + 1) * LANES)
        q_ref[0, :, cs] = (_half_rms(zq[:, cs], qg_ref[...], first) * (HEAD_DIM ** -0.5)).astype(BF16)
    zk = proj(2 * d + N_HEADS * LANES, KV_HEADS * LANES)
    for hh in range(KV_HEADS):
        cs = slice(hh * LANES, (hh + 1) * LANES)
        kn = _half_rms(zk[:, cs], kg_ref[...], first)
        k_ref[0, :, cs] = kn
        kb_ref[0, :, cs] = kn.astype(BF16)
    zv = proj(2 * d + (N_HEADS + KV_HEADS) * LANES, KV_HEADS * LANES)
    v_ref[0] = zv
    vb_ref[0] = zv.astype(BF16)
    base = 2 * d + (N_HEADS + 2 * KV_HEADS) * LANES
    ga_ref[0] = jax.nn.sigmoid(proj(base, d)).astype(BF16)
    gb_ref[0] = jax.nn.sigmoid(proj(base + d, d)).astype(BF16)


def _inproj(x, sc, sh, n1g, w_in, lng, lnb, qg, kg, ws, bs, *, sample, tm):
    nb, s, d = x.shape
    r = sc.shape[1]
    kw = KV_HEADS * LANES
    row = lambda b, i: (b, i, 0)
    mod = (lambda b, i: (b, i, 0)) if r == s else (lambda b, i: (b, 0, 0))
    rblk = tm if r == s else 1
    in_specs = [pl.BlockSpec((1, tm, d), row),
                pl.BlockSpec((1, rblk, d), mod),
                pl.BlockSpec((1, rblk, d), mod),
                _const_spec((1, d)),
                _const_spec(w_in.shape),
                _const_spec((1, d)), _const_spec((1, d)),
                _const_spec((1, LANES)), _const_spec((1, LANES)),
                _const_spec(ws.shape), _const_spec(bs.shape)]
    out_shapes = [jax.ShapeDtypeStruct((nb, s, d), BF16),
                  jax.ShapeDtypeStruct((nb, s, d), BF16),
                  jax.ShapeDtypeStruct((nb, s, kw), F32),
                  jax.ShapeDtypeStruct((nb, s, kw), F32),
                  jax.ShapeDtypeStruct((nb, s, kw), BF16),
                  jax.ShapeDtypeStruct((nb, s, kw), BF16),
                  jax.ShapeDtypeStruct((nb, s, d), BF16),
                  jax.ShapeDtypeStruct((nb, s, d), BF16)]
    if sample:
        out_shapes.append(jax.ShapeDtypeStruct((nb, s, d), F32))
    out_specs = [pl.BlockSpec((1, tm, o.shape[-1]), row) for o in out_shapes]
    return pl.pallas_call(
        functools.partial(_inproj_kernel, sample=sample),
        grid=(nb, s // tm),
        in_specs=in_specs,
        out_specs=out_specs,
        out_shape=out_shapes,
        compiler_params=pltpu.CompilerParams(dimension_semantics=("parallel", "parallel"),
                                             vmem_limit_bytes=VMEM_LIMIT),
        name="inproj_sample" if sample else "inproj",
    )(x, sc, sh, n1g, w_in, lng, lnb, qg, kg, ws, bs)


def _attn_kernel(lam_ref, q_ref, k_ref, v_ref, sg_ref, o_ref, qs_ref, m_ref, l_ref, acc_ref,
                 *, tq, out_scale):
    qi = pl.program_id(2)
    first = lax.broadcasted_iota(jnp.int32, (1, LANES), 1) < HEAD_DIM
    zero = jnp.zeros((), BF16)
    for g in range(GQA_GROUP):
        qg = q_ref[0, :, g * LANES:(g + 1) * LANES]
        qs_ref[(2 * g) * tq:(2 * g + 1) * tq, :] = jnp.where(first, qg, zero)
        qs_ref[(2 * g + 1) * tq:(2 * g + 2) * tq, :] = jnp.where(first, zero, qg)
    m_ref[...] = jnp.full(m_ref.shape, NEG, F32)
    l_ref[...] = jnp.zeros(l_ref.shape, F32)
    acc_ref[...] = jnp.zeros(acc_ref.shape, F32)

    def step(kj, masked):
        start = pl.multiple_of(kj * tq, tq)
        kt = k_ref[0, pl.ds(start, tq), :]
        vt = v_ref[0, pl.ds(start, tq), :]
        s = _nt_dot(qs_ref[...], kt)
        if masked:
            rowi = lax.broadcasted_iota(jnp.int32, s.shape, 0) & (tq - 1)
            coli = lax.broadcasted_iota(jnp.int32, s.shape, 1)
            s = jnp.where(coli <= rowi, s, NEG)
        m_prev = m_ref[...]
        m_new = jnp.maximum(m_prev, jnp.max(s, axis=-1, keepdims=True))
        alpha = jnp.exp(m_prev - m_new)
        p = jnp.exp(s - m_new)
        l_ref[...] = alpha * l_ref[...] + jnp.sum(p, axis=-1, keepdims=True)
        acc_ref[...] = alpha * acc_ref[...] + jnp.dot(p.astype(BF16), vt, preferred_element_type=F32)
        m_ref[...] = m_new

    def body(kj, carry):
        step(kj, False)
        return carry

    lax.fori_loop(0, qi, body, 0)
    step(qi, True)

    o_all = acc_ref[...] / l_ref[...]
    lam = lam_ref[0]
    for g in range(GQA_GROUP):
        og = o_all[(2 * g) * tq:(2 * g + 1) * tq] - lam * o_all[(2 * g + 1) * tq:(2 * g + 2) * tq]
        o_ref[0, :, g * LANES:(g + 1) * LANES] = (_rms(og) * sg_ref[...] * out_scale).astype(BF16)


def _attn_prompt(lam, q, kb, vb, sg, *, out_scale, tq):
    nb, s, d = q.shape
    gw = GQA_GROUP * LANES
    return pl.pallas_call(
        functools.partial(_attn_kernel, tq=tq, out_scale=out_scale),
        grid=(nb, KV_HEADS, s // tq),
        in_specs=[pl.BlockSpec(memory_space=pltpu.SMEM),
                  pl.BlockSpec((1, tq, gw), lambda b, h, i: (b, i, h)),
                  pl.BlockSpec((1, s, LANES), lambda b, h, i: (b, 0, h)),
                  pl.BlockSpec((1, s, LANES), lambda b, h, i: (b, 0, h)),
                  _const_spec((1, LANES))],
        out_specs=pl.BlockSpec((1, tq, gw), lambda b, h, i: (b, i, h)),
        out_shape=jax.ShapeDtypeStruct((nb, s, d), BF16),
        scratch_shapes=[pltpu.VMEM((2 * GQA_GROUP * tq, LANES), BF16),
                        pltpu.VMEM((2 * GQA_GROUP * tq, 1), F32),
                        pltpu.VMEM((2 * GQA_GROUP * tq, 1), F32),
                        pltpu.VMEM((2 * GQA_GROUP * tq, LANES), F32)],
        compiler_params=pltpu.CompilerParams(
            dimension_semantics=("parallel", "parallel", "arbitrary"),
            vmem_limit_bytes=VMEM_LIMIT),
        name="attn_prompt",
    )(lam, q, kb, vb, sg)


def _decode_kernel(pt_ref, lam_ref, q_ref, bias_ref, biasn_ref, kn_ref, vn_ref, sg_ref, *rest,
                   n_pg, out_scale):
    k_refs = rest[:n_pg]
    v_refs = rest[n_pg:2 * n_pg]
    o_ref = rest[2 * n_pg]
    m_ref, l_ref, acc_ref = rest[2 * n_pg + 1:]
    del pt_ref
    j = pl.program_id(1)
    half = N_HEADS

    @pl.when(j == 0)
    def _():
        m_ref[...] = jnp.full(m_ref.shape, NEG, F32)
        l_ref[...] = jnp.zeros(l_ref.shape, F32)
        acc_ref[...] = jnp.zeros(acc_ref.shape, F32)

    q16 = q_ref[0]

    def scores(kref, idx, bias):
        n = kref.shape[-2] // 2
        k0 = kref[idx + (pl.ds(0, n, stride=2), slice(None))].astype(BF16)
        k1 = kref[idx + (pl.ds(1, n, stride=2), slice(None))].astype(BF16)
        s0 = _nt_dot(q16, k0)[:half]
        s1 = _nt_dot(q16, k1)[half:]
        return jnp.concatenate([s0, s1], axis=0) + bias

    def update(s_list, v_list):
        s = jnp.concatenate(s_list, axis=1) if len(s_list) > 1 else s_list[0]
        m_prev = m_ref[...]
        m_new = jnp.maximum(m_prev, jnp.max(s, axis=-1, keepdims=True))
        alpha = jnp.exp(m_prev - m_new)
        p = jnp.exp(s - m_new)
        l_ref[...] = alpha * l_ref[...] + jnp.sum(p, axis=-1, keepdims=True)
        w = s_list[0].shape[1]
        pv = None
        for i, vv in enumerate(v_list):
            t = jnp.dot(p[:, i * w:(i + 1) * w].astype(BF16), vv, preferred_element_type=F32)
            pv = t if pv is None else pv + t
        acc_ref[...] = alpha * acc_ref[...] + pv
        m_ref[...] = m_new

    bias = bias_ref[...]
    update([scores(kr, (0, 0), bias) for kr in k_refs],
           [vr[0, 0].astype(BF16) for vr in v_refs])

    @pl.when(j == pl.num_programs(1) - 1)
    def _():
        update([scores(kn_ref, (0,), biasn_ref[...])], [vn_ref[0].astype(BF16)])
        o16 = acc_ref[...] / l_ref[...]
        og = o16[:half] - lam_ref[0] * o16[half:]
        o_ref[0] = _rms(og) * sg_ref[...] * out_scale


def _attn_decode(layer, page_table, lam, q16, bias, bias_new, k_new, v_new, sg, ck, cv, *, out_scale):
    db, n_pages = page_table.shape
    n_pg = PAGES_PER_STEP
    rows_k, rows_v = ck.shape[2], cv.shape[2]
    pt_flat = page_table.reshape(-1)

    def page_map(i):
        return lambda b, j, pt: (layer, pt[b * n_pages + j * n_pg + i], 0, 0)

    per_b = lambda b, j, pt: (b, 0, 0)
    const2 = lambda b, j, pt: (0, 0)
    in_specs = [pl.BlockSpec(memory_space=pltpu.SMEM),
                pl.BlockSpec((1, 2 * N_HEADS, HEAD_DIM), per_b),
                pl.BlockSpec(bias.shape, const2),
                pl.BlockSpec(bias_new.shape, const2),
                pl.BlockSpec((1, rows_k, HEAD_DIM), per_b),
                pl.BlockSpec((1, rows_v, V_DIM), per_b),
                pl.BlockSpec((1, LANES), const2)]
    in_specs += [pl.BlockSpec((1, 1, rows_k, HEAD_DIM), page_map(i)) for i in range(n_pg)]
    in_specs += [pl.BlockSpec((1, 1, rows_v, V_DIM), page_map(i)) for i in range(n_pg)]
    grid_spec = pltpu.PrefetchScalarGridSpec(
        num_scalar_prefetch=1,
        grid=(db, n_pages // n_pg),
        in_specs=in_specs,
        out_specs=pl.BlockSpec((1, N_HEADS, V_DIM), per_b),
        scratch_shapes=[pltpu.VMEM((2 * N_HEADS, 1), F32),
                        pltpu.VMEM((2 * N_HEADS, 1), F32),
                        pltpu.VMEM((2 * N_HEADS, V_DIM), F32)])
    return pl.pallas_call(
        functools.partial(_decode_kernel, n_pg=n_pg, out_scale=out_scale),
        grid_spec=grid_spec,
        out_shape=jax.ShapeDtypeStruct((db, N_HEADS, V_DIM), F32),
        compiler_params=pltpu.CompilerParams(dimension_semantics=("parallel", "arbitrary"),
                                             vmem_limit_bytes=VMEM_LIMIT),
        name="attn_decode",
    )(pt_flat, lam, q16, bias, bias_new, k_new, v_new, sg, *([ck] * n_pg), *([cv] * n_pg))


def _split_bf16(x):
    hi = x.astype(BF16)
    return hi, (x - hi.astype(F32)).astype(BF16)


def _merge_kernel(mixed_ref, o_ref, ga_ref, gb_ref, x_ref, g1_ref, sc_ref, sh_ref, wa_ref, wb_ref,
                  wo_ref, n2g_ref, rwh_ref, rwl_ref, rb_ref, xo_ref, h2_ref, lg_ref):
    a = jnp.dot(mixed_ref[0], wa_ref[...], preferred_element_type=F32)
    b = jnp.dot(o_ref[0], wb_ref[...], preferred_element_type=F32)
    merged = ga_ref[0].astype(F32) * a + gb_ref[0].astype(F32) * b
    xo = x_ref[0] + g1_ref[0] * jnp.dot(merged.astype(BF16), wo_ref[...], preferred_element_type=F32)
    xo_ref[0] = xo
    h2 = _rms(xo) * n2g_ref[...] * (1.0 + sc_ref[0]) + sh_ref[0]
    hi, lo = _split_bf16(h2)
    h2_ref[0] = hi
    lg_ref[0] = (jnp.dot(hi, rwh_ref[...], preferred_element_type=F32)
                 + jnp.dot(lo, rwh_ref[...], preferred_element_type=F32)
                 + jnp.dot(hi, rwl_ref[...], preferred_element_type=F32)
                 + rb_ref[...])


def _merge(mixed, o, ga, gb, x, g1, sc, sh, wa, wb, wo, n2g, rwh, rwl, rb, *, tm):
    nb, s, d = x.shape
    r = g1.shape[1]
    row = lambda b, i: (b, i, 0)
    mod = (lambda b, i: (b, i, 0)) if r == s else (lambda b, i: (b, 0, 0))
    rblk = tm if r == s else 1
    tok = pl.BlockSpec((1, tm, d), row)
    mods = pl.BlockSpec((1, rblk, d), mod)
    return pl.pallas_call(
        _merge_kernel,
        grid=(nb, s // tm),
        in_specs=[tok, tok, tok, tok, tok, mods, mods, mods,
                  _const_spec((d, d)), _const_spec((d, d)), _const_spec((d, d)),
                  _const_spec((1, d)), _const_spec((d, LANES)), _const_spec((d, LANES)),
                  _const_spec((1, LANES))],
        out_specs=[tok, tok, pl.BlockSpec((1, tm, LANES), row)],
        out_shape=[jax.ShapeDtypeStruct((nb, s, d), F32),
                   jax.ShapeDtypeStruct((nb, s, d), BF16),
                   jax.ShapeDtypeStruct((nb, s, LANES), F32)],
        compiler_params=pltpu.CompilerParams(dimension_semantics=("parallel", "parallel"),
                                             vmem_limit_bytes=VMEM_LIMIT),
        name="merge",
    )(mixed, o, ga, gb, x, g1, sc, sh, wa, wb, wo, n2g, rwh, rwl, rb)


def _expert_kernel(be_ref, nu_ref, x_ref, w1_ref, b1_ref, w2_ref, b2_ref, y_ref):
    del be_ref
    f = w2_ref.shape[1]

    @pl.when(pl.program_id(0) < nu_ref[0])
    def _():
        z = jnp.dot(x_ref[...], w1_ref[0], preferred_element_type=F32) + b1_ref[0]
        glu = jnp.minimum(z[:, :f], SWIGLU_LIMIT)
        lin = jnp.clip(z[:, f:], -SWIGLU_LIMIT, SWIGLU_LIMIT)
        act = glu * jax.nn.sigmoid(SWIGLU_ALPHA * glu) * (lin + 1.0)
        y_ref[...] = jnp.dot(act.astype(BF16), w2_ref[0], preferred_element_type=F32) + b2_ref[0]

    @pl.when(pl.program_id(0) >= nu_ref[0])
    def _():
        y_ref[...] = jnp.zeros(y_ref.shape, F32)


def _experts(block_expert, n_used, xs, w1, b1, w2, b2, *, tb):
    n_slots, d = xs.shape
    n_e, _, f2 = w1.shape
    f = w2.shape[1]
    grid_spec = pltpu.PrefetchScalarGridSpec(
        num_scalar_prefetch=2,
        grid=(n_slots // tb,),
        in_specs=[pl.BlockSpec((tb, d), lambda i, be, nu: (i, 0)),
                  pl.BlockSpec((1, d, f2), lambda i, be, nu: (be[i], 0, 0)),
                  pl.BlockSpec((1, 1, f2), lambda i, be, nu: (be[i], 0, 0)),
                  pl.BlockSpec((1, f, d), lambda i, be, nu: (be[i], 0, 0)),
                  pl.BlockSpec((1, 1, d), lambda i, be, nu: (be[i], 0, 0))],
        out_specs=pl.BlockSpec((tb, d), lambda i, be, nu: (i, 0)))
    return pl.pallas_call(
        _expert_kernel,
        grid_spec=grid_spec,
        out_shape=jax.ShapeDtypeStruct((n_slots, d), F32),
        compiler_params=pltpu.CompilerParams(dimension_semantics=("arbitrary",),
                                             vmem_limit_bytes=VMEM_LIMIT),
        name="experts",
    )(block_expert, n_used, xs, w1, b1.reshape(n_e, 1, f2), w2, b2.reshape(n_e, 1, d))


def _combine_kernel(yg_ref, gate_ref, x_ref, g2_ref, o_ref):
    gates = gate_ref[0]
    y = yg_ref[0, 0] * gates[:, 0:1]
    for k in range(1, TOP_K):
        y = y + yg_ref[k, 0] * gates[:, k:k + 1]
    o_ref[0] = x_ref[0] + g2_ref[0] * y


def _combine(yg, gates, x, g2, *, tm):
    nb, s, d = x.shape
    r = g2.shape[1]
    row = lambda b, i: (b, i, 0)
    mod = (lambda b, i: (b, i, 0)) if r == s else (lambda b, i: (b, 0, 0))
    rblk = tm if r == s else 1
    return pl.pallas_call(
        _combine_kernel,
        grid=(nb, s // tm),
        in_specs=[pl.BlockSpec((TOP_K, 1, tm, d), lambda b, i: (0, b, i, 0)),
                  pl.BlockSpec((1, tm, TOP_K), row),
                  pl.BlockSpec((1, tm, d), row),
                  pl.BlockSpec((1, rblk, d), mod)],
        out_specs=pl.BlockSpec((1, tm, d), row),
        out_shape=jax.ShapeDtypeStruct((nb, s, d), F32),
        compiler_params=pltpu.CompilerParams(dimension_semantics=("parallel", "parallel"),
                                             vmem_limit_bytes=VMEM_LIMIT),
        name="combine",
    )(yg, gates, x, g2)


def _moe(h2, logits, x, g2, w1, b1, w2, b2, *, tb, tm):
    nb, s, d = h2.shape
    t = nb * s
    n_assign = t * TOP_K
    top_val, top_idx = lax.top_k(logits.reshape(t, LANES)[:, :N_EXPERTS], TOP_K)
    gates = jax.nn.softmax(top_val, axis=-1)
    e_flat = top_idx.reshape(n_assign)
    onehot = (e_flat[:, None] == jnp.arange(N_EXPERTS, dtype=jnp.int32)[None, :]).astype(jnp.int32)
    csum = jnp.cumsum(onehot, axis=0)
    rank = jnp.sum(onehot * csum, axis=1) - 1
    counts = csum[-1]
    padded = (counts + tb - 1) // tb * tb
    pad_ends = jnp.cumsum(padded)
    pad_starts = pad_ends - padded
    dest = pad_starts[e_flat] + rank
    n_blocks = -(-n_assign // tb) + N_EXPERTS
    n_slots = n_blocks * tb
    tok_flat = jnp.repeat(jnp.arange(t, dtype=jnp.int32), TOP_K)
    slot_tok = jnp.full((n_slots,), t, jnp.int32).at[dest].set(tok_flat)
    block_expert = jnp.minimum(
        jnp.searchsorted(pad_ends, jnp.arange(n_blocks, dtype=jnp.int32) * tb, side='right'),
        N_EXPERTS - 1).astype(jnp.int32)
    n_used = (pad_ends[-1:] // tb).astype(jnp.int32)
    h_pad = jnp.concatenate([h2.reshape(t, d), jnp.zeros((1, d), h2.dtype)], axis=0)
    xs = h_pad[slot_tok]
    yb = _experts(block_expert, n_used, xs, w1, b1, w2, b2, tb=tb)
    yg = yb[dest.reshape(t, TOP_K).T].reshape(TOP_K, nb, s, d)
    return _combine(yg, gates.reshape(nb, s, TOP_K), x, g2, tm=tm)


def _lambda_init(layer):
    return 0.8 - 0.6 * math.exp(-0.3 * layer)


def _decode_bias(n_pos, visible):
    col = jnp.arange(n_pos * KV_HEADS, dtype=jnp.int32)[None, :]
    row = jnp.arange(2 * N_HEADS, dtype=jnp.int32)[:, None]
    ok = ((col % KV_HEADS) == ((row % N_HEADS) // GQA_GROUP)) & ((col // KV_HEADS) < visible)
    return jnp.where(ok, 0.0, NEG).astype(F32)


def kernel(x_prompt, x_sample, c_prompt, c_sample, cache_k, cache_v, page_table, ada_w, ada_b, norm1_g, norm2_g, w_in, q_norm_g, k_norm_g, lambda_q1, lambda_k1, lambda_q2, lambda_k2, subln_g, ln_v_g, ln_v_b, w_spatial, b_spatial, w_branch_a, w_branch_b, w_out, router_w, router_b, w1, b1, w2, b2):
    depth = ada_w.shape[0]
    nb, seq, d = x_prompt.shape
    db, ds, _ = x_sample.shape
    assert ds == 1, "sampled tokens open a fresh chunk one row at a time"
    n_phys, page = cache_k.shape[1], cache_k.shape[2]
    f = w2.shape[2]

    mods = _ada_mods(jnp.concatenate([c_prompt, c_sample], axis=0), ada_w, ada_b)
    ck = cache_k.reshape(depth, n_phys, page * KV_HEADS * 2, HEAD_DIM)
    cv = cache_v.reshape(depth, n_phys, page * KV_HEADS, V_DIM)
    bias = _decode_bias(page, page)
    bias_new = _decode_bias(page, ds)
    tri = jnp.tril(jnp.ones((CHUNK, CHUNK), F32))

    yp = x_prompt
    ys = x_sample.reshape(1, db, d)
    kp_l, vp_l, ks_l, vs_l, cv_l = [], [], [], [], []
    for l in range(depth):
        lam_init = _lambda_init(l)
        lam = (jnp.exp(jnp.sum(lambda_q1[l] * lambda_k1[l])) - jnp.exp(jnp.sum(lambda_q2[l] * lambda_k2[l]))
               + lam_init).reshape(1).astype(F32)
        m6 = mods[l].reshape(nb + db, 6, d)
        mp = [m6[:nb, i][:, None, :] for i in range(6)]
        msm = [m6[nb:, i][None, :, :] for i in range(6)]
        w_in_b = w_in[l].astype(BF16)
        n1g = norm1_g[l][None, :]
        n2g = norm2_g[l][None, :]
        lng, lnb = ln_v_g[l][None, :], ln_v_b[l][None, :]
        qg = jnp.tile(q_norm_g[l], 2)[None, :]
        kg = jnp.tile(k_norm_g[l], 2)[None, :]
        sg = subln_g[l][None, :]
        ws_p = (w_spatial[l] * tri).astype(BF16)
        bs_p = jnp.repeat(b_spatial[l].T, LANES, axis=1)
        ws_s = jnp.repeat(w_spatial[l][:, 0, 0], LANES)[None, :]
        bs_s = jnp.repeat(b_spatial[l][:, 0], LANES)[None, :]
        wa, wb, wo = (w_branch_a[l].astype(BF16), w_branch_b[l].astype(BF16), w_out[l].astype(BF16))
        rw = jnp.pad(router_w[l], ((0, 0), (0, LANES - N_EXPERTS)))
        rwh, rwl = _split_bf16(rw)
        rb = jnp.pad(router_b[l], (0, LANES - N_EXPERTS))[None, :]
        w1p = jnp.concatenate([w1[l][:, :, 0::2], w1[l][:, :, 1::2]], axis=-1).astype(BF16)
        b1p = jnp.concatenate([b1[l][:, 0::2], b1[l][:, 1::2]], axis=-1)
        w2p = w2[l].astype(BF16)
        out_scale = 1.0 - lam_init

        mixed, q, k, v, kb, vb, ga, gb = _inproj(yp, mp[1], mp[0], n1g, w_in_b, lng, lnb, qg, kg,
                                                 ws_p, bs_p, sample=False, tm=TM_DENSE)
        o = _attn_prompt(lam, q, kb, vb, sg, out_scale=out_scale, tq=TQ)
        xo, h2, lg = _merge(mixed, o, ga, gb, yp, mp[2], mp[4], mp[3], wa, wb, wo, n2g, rwh, rwl, rb,
                            tm=TM_DENSE)
        yp = _moe(h2, lg, xo, mp[5], w1p, b1p, w2p, b2[l], tb=TB_PROMPT, tm=TM_DENSE)
        kp_l.append(k.reshape(nb, seq, KV_HEADS, 2, HEAD_DIM))
        vp_l.append(v.reshape(nb, seq, KV_HEADS, V_DIM))

        mixed, q, k, v, kb, vb, ga, gb, cvs = _inproj(ys, msm[1], msm[0], n1g, w_in_b, lng, lnb, qg, kg,
                                                      ws_s, bs_s, sample=True, tm=db)
        q16 = q.reshape(db, N_HEADS, 2, HEAD_DIM).transpose(0, 2, 1, 3).reshape(db, 2 * N_HEADS, HEAD_DIM)
        k_new = jnp.pad(k.reshape(db, KV_HEADS * 2, HEAD_DIM), ((0, 0), (0, (page - 1) * KV_HEADS * 2), (0, 0)))
        v_new = jnp.pad(v.reshape(db, KV_HEADS, V_DIM), ((0, 0), (0, (page - 1) * KV_HEADS), (0, 0)))
        o = _attn_decode(l, page_table, lam, q16, bias, bias_new, k_new, v_new, sg, ck, cv,
                         out_scale=out_scale)
        o = o.reshape(1, db, d).astype(BF16)
        xo, h2, lg = _merge(mixed, o, ga, gb, ys, msm[2], msm[4], msm[3], wa, wb, wo, n2g, rwh, rwl, rb,
                            tm=db)
        ys = _moe(h2, lg, xo, msm[5], w1p, b1p, w2p, b2[l], tb=TB_SAMPLE, tm=db)
        ks_l.append(k.reshape(db, ds, KV_HEADS, 2, HEAD_DIM))
        vs_l.append(v.reshape(db, ds, KV_HEADS, V_DIM))
        cv_l.append(cvs.reshape(db, ds, d))

    return (yp, ys.reshape(db, ds, d), jnp.stack(kp_l), jnp.stack(vp_l), jnp.stack(ks_l),
            jnp.stack(vs_l), jnp.stack(cv_l))
```

```python
import functools
import math

import jax
import jax.numpy as jnp
from jax import lax
from jax.experimental import pallas as pl
from jax.experimental.pallas import tpu as pltpu

F32 = jnp.float32
BF16 = jnp.bfloat16

LANES = 128
MXU_DIM = 256
HEAD_DIM = 64
N_HEADS = 8
KV_HEADS = 4
GQA_GROUP = N_HEADS // KV_HEADS
V_DIM = 2 * HEAD_DIM
CHUNK = 128
A_GROUPS = 8
N_EXPERTS = 32
TOP_K = 4
SWIGLU_ALPHA = 1.702
SWIGLU_LIMIT = 7.0
NEG = -1e30

TM_DENSE = 256
TQ = 256
TK = 512
PAGES_PER_STEP = 8
TB_PROMPT = 256
TB_SAMPLE = 16
VMEM_LIMIT = 48 * 1024 * 1024


def _const_spec(shape):
    nd = len(shape)
    return pl.BlockSpec(shape, lambda *_: (0,) * nd, pipeline_mode=pl.Buffered(1))


def _gelu(x):
    return x * (lax.erf(x * (1.0 / math.sqrt(2.0))) + 1.0) * 0.5


def _rms(x, eps=1e-6):
    return x * lax.rsqrt(jnp.mean(x * x, axis=-1, keepdims=True) + eps)


def _ada_kernel(c_ref, w_ref, b_ref, o_ref):
    c = c_ref[...]
    sc = (c * jax.nn.sigmoid(c)).astype(BF16)
    o_ref[0] = jnp.dot(sc, w_ref[0].astype(BF16), preferred_element_type=F32) + b_ref[0]


def _ada_mods(c_all, ada_w, ada_b):
    depth, d, n = ada_w.shape
    r = c_all.shape[0]
    tn = 512
    return pl.pallas_call(
        _ada_kernel,
        grid=(depth, n // tn),
        in_specs=[pl.BlockSpec((r, d), lambda l, j: (0, 0)),
                  pl.BlockSpec((1, d, tn), lambda l, j: (l, 0, j)),
                  pl.BlockSpec((1, 1, tn), lambda l, j: (l, 0, j))],
        out_specs=pl.BlockSpec((1, r, tn), lambda l, j: (l, 0, j)),
        out_shape=jax.ShapeDtypeStruct((depth, r, n), F32),
        compiler_params=pltpu.CompilerParams(dimension_semantics=("parallel", "parallel")),
        name="ada",
    )(c_all, ada_w, ada_b.reshape(depth, 1, n))


def _half_rms(t, g_row, first):
    s = t * t
    s1 = jnp.sum(jnp.where(first, s, 0.0), axis=-1, keepdims=True)
    s2 = jnp.sum(jnp.where(first, 0.0, s), axis=-1, keepdims=True)
    r = jnp.where(first, lax.rsqrt(s1 * (1.0 / HEAD_DIM) + 1e-6),
                  lax.rsqrt(s2 * (1.0 / HEAD_DIM) + 1e-6))
    return t * r * g_row


def _inproj_kernel(x_ref, sc_ref, sh_ref, n1g_ref, w_ref, lng_ref, lnb_ref, qg_ref, kg_ref,
                   ws_ref, bs_ref, mixed_ref, q_ref, k_ref, v_ref, kb_ref, vb_ref, ga_ref, gb_ref,
                   *cv_ref, sample):
    d = x_ref.shape[-1]
    tm = x_ref.shape[1]
    x = x_ref[0]
    h = (_rms(x) * n1g_ref[...] * (1.0 + sc_ref[0]) + sh_ref[0]).astype(BF16)

    def proj(lo, width):
        return jnp.dot(h, w_ref[:, lo:lo + width], preferred_element_type=F32)

    u = _gelu(proj(0, d))
    va = _gelu(proj(d, d))
    dv = va - jnp.mean(va, axis=-1, keepdims=True)
    van = dv * lax.rsqrt(jnp.mean(dv * dv, axis=-1, keepdims=True) + 1e-5) * lng_ref[...] + lnb_ref[...]
    if sample:
        cv_ref[0][0] = van
        mixed_ref[0] = (u * (van * ws_ref[...] + bs_ref[...])).astype(BF16)
    else:
        vb16 = van.astype(BF16)
        for c in range(tm // CHUNK):
            rs = slice(c * CHUNK, (c + 1) * CHUNK)
            for g in range(A_GROUPS):
                cs = slice(g * LANES, (g + 1) * LANES)
                mix = jnp.dot(ws_ref[g], vb16[rs, cs], preferred_element_type=F32) + bs_ref[:, cs]
                mixed_ref[0, rs, cs] = (u[rs, cs] * mix).astype(BF16)

    first = lax.broadcasted_iota(jnp.int32, (1, LANES), 1) < HEAD_DIM
    zq = proj(2 * d, N_HEADS * LANES)
    for hh in range(N_HEADS):
        cs = slice(hh * LANES, (hh + 1) * LANES)
        q_ref[0, :, cs] = (_half_rms(zq[:, cs], qg_ref[...], first) * (HEAD_DIM ** -0.5)).astype(BF16)
    zk = proj(2 * d + N_HEADS * LANES, KV_HEADS * LANES)
    for hh in range(KV_HEADS):
        cs = slice(hh * LANES, (hh + 1) * LANES)
        kn = _half_rms(zk[:, cs], kg_ref[...], first)
        k_ref[0, :, cs] = kn
        kb_ref[0, :, cs] = kn.astype(BF16)
    zv = proj(2 * d + (N_HEADS + KV_HEADS) * LANES, KV_HEADS * LANES)
    v_ref[0] = zv
    vb_ref[0] = zv.astype(BF16)
    base = 2 * d + (N_HEADS + 2 * KV_HEADS) * LANES
    ga_ref[0] = jax.nn.sigmoid(proj(base, d)).astype(BF16)
    gb_ref[0] = jax.nn.sigmoid(proj(base + d, d)).astype(BF16)


def _inproj(x, sc, sh, n1g, w_in, lng, lnb, qg, kg, ws, bs, *, sample, tm):
    nb, s, d = x.shape
    r = sc.shape[1]
    kw = KV_HEADS * LANES
    row = lambda b, i: (b, i, 0)
    mod = (lambda b, i: (b, i, 0)) if r == s else (lambda b, i: (b, 0, 0))
    rblk = tm if r == s else 1
    in_specs = [pl.BlockSpec((1, tm, d), row),
                pl.BlockSpec((1, rblk, d), mod),
                pl.BlockSpec((1, rblk, d), mod),
                _const_spec((1, d)),
                _const_spec(w_in.shape),
                _const_spec((1, d)), _const_spec((1, d)),
                _const_spec((1, LANES)), _const_spec((1, LANES)),
                _const_spec(ws.shape), _const_spec(bs.shape)]
    out_shapes = [jax.ShapeDtypeStruct((nb, s, d), BF16),
                  jax.ShapeDtypeStruct((nb, s, d), BF16),
                  jax.ShapeDtypeStruct((nb, s, kw), F32),
                  jax.ShapeDtypeStruct((nb, s, kw), F32),
                  jax.ShapeDtypeStruct((nb, s, kw), BF16),
                  jax.ShapeDtypeStruct((nb, s, kw), BF16),
                  jax.ShapeDtypeStruct((nb, s, d), BF16),
                  jax.ShapeDtypeStruct((nb, s, d), BF16)]
    if sample:
        out_shapes.append(jax.ShapeDtypeStruct((nb, s, d), F32))
    out_specs = [pl.BlockSpec((1, tm, o.shape[-1]), row) for o in out_shapes]
    return pl.pallas_call(
        functools.partial(_inproj_kernel, sample=sample),
        grid=(nb, s // tm),
        in_specs=in_specs,
        out_specs=out_specs,
        out_shape=out_shapes,
        compiler_params=pltpu.CompilerParams(dimension_semantics=("parallel", "parallel"),
                                             vmem_limit_bytes=VMEM_LIMIT),
        name="inproj_sample" if sample else "inproj",
    )(x, sc, sh, n1g, w_in, lng, lnb, qg, kg, ws, bs)


def _attn_kernel(lam_ref, q_ref, k_ref, v_ref, sg_ref, o_ref, qs_ref, vt_ref, m_ref, l_ref, acc_ref,
                 *, tq, tk, out_scale):
    qi = pl.program_id(2)
    n_kt = vt_ref.shape[0]

    @pl.when(qi == 0)
    def _():
        def fill(j, carry):
            start = pl.multiple_of(j * tk, tk)
            vt_ref[j] = v_ref[0, pl.ds(start, tk), :].astype(F32).T.astype(BF16)
            return carry
        lax.fori_loop(0, n_kt, fill, 0)

    first = lax.broadcasted_iota(jnp.int32, (LANES, 1), 0) < HEAD_DIM
    for g in range(GQA_GROUP):
        qt = q_ref[0, :, g * LANES:(g + 1) * LANES].astype(F32).T
        qs_ref[:, (2 * g) * tq:(2 * g + 1) * tq] = jnp.where(first, qt, 0.0).astype(BF16)
        qs_ref[:, (2 * g + 1) * tq:(2 * g + 2) * tq] = jnp.where(first, 0.0, qt).astype(BF16)
    m_ref[...] = jnp.full(m_ref.shape, NEG, F32)
    l_ref[...] = jnp.zeros(l_ref.shape, F32)
    acc_ref[...] = jnp.zeros(acc_ref.shape, F32)

    def step(kj, masked):
        start = pl.multiple_of(kj * tk, tk)
        kt = k_ref[0, pl.ds(start, tk), :]
        s = jnp.dot(kt, qs_ref[...], preferred_element_type=F32)
        if masked:
            kpos = start + lax.broadcasted_iota(jnp.int32, s.shape, 0)
            qpos = qi * tq + (lax.broadcasted_iota(jnp.int32, s.shape, 1) & (tq - 1))
            s = jnp.where(kpos <= qpos, s, NEG)
        m_prev = m_ref[...]
        m_new = jnp.maximum(m_prev, jnp.max(s, axis=0, keepdims=True))
        alpha = jnp.exp(m_prev - m_new)
        p = jnp.exp(s - m_new)
        l_ref[...] = alpha * l_ref[...] + jnp.sum(p, axis=0, keepdims=True)
        acc_ref[...] = alpha * acc_ref[...] + jnp.dot(vt_ref[kj], p.astype(BF16),
                                                      preferred_element_type=F32)
        m_ref[...] = m_new

    def body(kj, carry):
        step(kj, False)
        return carry

    n_full = (qi * tq) // tk
    lax.fori_loop(0, n_full, body, 0)
    step(n_full, True)

    o_all = acc_ref[...] / l_ref[...]
    lam = lam_ref[0]
    for g in range(GQA_GROUP):
        og = o_all[:, (2 * g) * tq:(2 * g + 1) * tq] - lam * o_all[:, (2 * g + 1) * tq:(2 * g + 2) * tq]
        og = og * lax.rsqrt(jnp.mean(og * og, axis=0, keepdims=True) + 1e-6) * sg_ref[...] * out_scale
        o_ref[0, :, g * LANES:(g + 1) * LANES] = og.T.astype(BF16)


def _attn_prompt(lam, q, kb, vb, sg_col, *, out_scale, tq, tk):
    nb, s, d = q.shape
    gw = GQA_GROUP * LANES
    m = 2 * GQA_GROUP * tq
    return pl.pallas_call(
        functools.partial(_attn_kernel, tq=tq, tk=tk, out_scale=out_scale),
        grid=(nb, KV_HEADS, s // tq),
        in_specs=[pl.BlockSpec(memory_space=pltpu.SMEM),
                  pl.BlockSpec((1, tq, gw), lambda b, h, i: (b, i, h)),
                  pl.BlockSpec((1, s, LANES), lambda b, h, i: (b, 0, h)),
                  pl.BlockSpec((1, s, LANES), lambda b, h, i: (b, 0, h)),
                  _const_spec((LANES, 1))],
        out_specs=pl.BlockSpec((1, tq, gw), lambda b, h, i: (b, i, h)),
        out_shape=jax.ShapeDtypeStruct((nb, s, d), BF16),
        scratch_shapes=[pltpu.VMEM((LANES, m), BF16),
                        pltpu.VMEM((s // tk, LANES, tk), BF16),
                        pltpu.VMEM((1, m), F32),
                        pltpu.VMEM((1, m), F32),
                        pltpu.VMEM((LANES, m), F32)],
        compiler_params=pltpu.CompilerParams(
            dimension_semantics=("parallel", "parallel", "arbitrary"),
            vmem_limit_bytes=VMEM_LIMIT),
        name="attn_prompt",
    )(lam, q, kb, vb, sg_col)


def _decode_kernel(pt_ref, lam_ref, q_ref, bias_ref, biasn_ref, kn_ref, vn_ref, sg_ref, *rest,
                   n_pg, out_scale):
    k_refs = rest[:n_pg]
    v_refs = rest[n_pg:2 * n_pg]
    o_ref = rest[2 * n_pg]
    m_ref, l_ref, acc_ref = rest[2 * n_pg + 1:]
    del pt_ref
    j = pl.program_id(1)
    half = N_HEADS

    @pl.when(j == 0)
    def _():
        m_ref[...] = jnp.full(m_ref.shape, NEG, F32)
        l_ref[...] = jnp.zeros(l_ref.shape, F32)
        acc_ref[...] = jnp.zeros(acc_ref.shape, F32)

    q16 = q_ref[0]

    def scores(kref, idx, bias):
        parts = []
        for mp in range(2):
            kcat = jnp.concatenate([kref[idx + (kvh, mp)] for kvh in range(KV_HEADS)], axis=1)
            sm = jnp.dot(q16, kcat.astype(BF16), preferred_element_type=F32)
            parts.append(sm[mp * half:(mp + 1) * half])
        return jnp.concatenate(parts, axis=0) + bias

    def values(vref, idx):
        n = vref.shape[-2] // KV_HEADS
        return jnp.concatenate([vref[idx + (pl.ds(kvh, n, stride=KV_HEADS), slice(None))]
                                for kvh in range(KV_HEADS)], axis=0).astype(BF16)

    def update(s_list, v_list):
        s = jnp.concatenate(s_list, axis=1) if len(s_list) > 1 else s_list[0]
        m_prev = m_ref[...]
        m_new = jnp.maximum(m_prev, jnp.max(s, axis=-1, keepdims=True))
        alpha = jnp.exp(m_prev - m_new)
        p = jnp.exp(s - m_new)
        l_ref[...] = alpha * l_ref[...] + jnp.sum(p, axis=-1, keepdims=True)
        w = s_list[0].shape[1]
        pv = None
        for i, vv in enumerate(v_list):
            t = jnp.dot(p[:, i * w:(i + 1) * w].astype(BF16), vv, preferred_element_type=F32)
            pv = t if pv is None else pv + t
        acc_ref[...] = alpha * acc_ref[...] + pv
        m_ref[...] = m_new

    bias = bias_ref[...]
    update([scores(kr, (0, 0), bias) for kr in k_refs], [values(vr, (0, 0)) for vr in v_refs])

    @pl.when(j == pl.num_programs(1) - 1)
    def _():
        update([scores(kn_ref, (0,), biasn_ref[...])], [values(vn_ref, (0,))])
        o16 = acc_ref[...] / l_ref[...]
        og = o16[:half] - lam_ref[0] * o16[half:]
        o_ref[0] = _rms(og) * sg_ref[...] * out_scale


def _attn_decode(layer, page_table, lam, q16, bias, bias_new, k_new, v_new, sg, ck, cv, *, out_scale):
    db, n_pages = page_table.shape
    n_pg = PAGES_PER_STEP
    page = ck.shape[-1]
    rows_v = cv.shape[2]
    pt_flat = page_table.reshape(-1)

    def kmap(i):
        return lambda b, j, pt: (layer, pt[b * n_pages + j * n_pg + i], 0, 0, 0, 0)

    def vmap(i):
        return lambda b, j, pt: (layer, pt[b * n_pages + j * n_pg + i], 0, 0)

    per_b = lambda b, j, pt: (b, 0, 0)
    const2 = lambda b, j, pt: (0, 0)
    kblk = (KV_HEADS, 2, HEAD_DIM, page)
    in_specs = [pl.BlockSpec(memory_space=pltpu.SMEM),
                pl.BlockSpec((1, 2 * N_HEADS, HEAD_DIM), per_b),
                pl.BlockSpec(bias.shape, const2),
                pl.BlockSpec(bias_new.shape, const2),
                pl.BlockSpec((1,) + kblk, lambda b, j, pt: (b, 0, 0, 0, 0)),
                pl.BlockSpec((1, rows_v, V_DIM), per_b),
                pl.BlockSpec((1, LANES), const2)]
    in_specs += [pl.BlockSpec((1, 1) + kblk, kmap(i)) for i in range(n_pg)]
    in_specs += [pl.BlockSpec((1, 1, rows_v, V_DIM), vmap(i)) for i in range(n_pg)]
    grid_spec = pltpu.PrefetchScalarGridSpec(
        num_scalar_prefetch=1,
        grid=(db, n_pages // n_pg),
        in_specs=in_specs,
        out_specs=pl.BlockSpec((1, N_HEADS, V_DIM), per_b),
        scratch_shapes=[pltpu.VMEM((2 * N_HEADS, 1), F32),
                        pltpu.VMEM((2 * N_HEADS, 1), F32),
                        pltpu.VMEM((2 * N_HEADS, V_DIM), F32)])
    return pl.pallas_call(
        functools.partial(_decode_kernel, n_pg=n_pg, out_scale=out_scale),
        grid_spec=grid_spec,
        out_shape=jax.ShapeDtypeStruct((db, N_HEADS, V_DIM), F32),
        compiler_params=pltpu.CompilerParams(dimension_semantics=("parallel", "arbitrary"),
                                             vmem_limit_bytes=VMEM_LIMIT),
        name="attn_decode",
    )(pt_flat, lam, q16, bias, bias_new, k_new, v_new, sg, *([ck] * n_pg), *([cv] * n_pg))


def _split_bf16(x):
    hi = x.astype(BF16)
    return hi, (x - hi.astype(F32)).astype(BF16)


def _merge_kernel(mixed_ref, o_ref, ga_ref, gb_ref, x_ref, g1_ref, sc_ref, sh_ref, wa_ref, wb_ref,
                  wo_ref, n2g_ref, rwh_ref, rwl_ref, rb_ref, xo_ref, h2_ref, lg_ref):
    a = jnp.dot(mixed_ref[0], wa_ref[...], preferred_element_type=F32)
    b = jnp.dot(o_ref[0], wb_ref[...], preferred_element_type=F32)
    merged = ga_ref[0].astype(F32) * a + gb_ref[0].astype(F32) * b
    xo = x_ref[0] + g1_ref[0] * jnp.dot(merged.astype(BF16), wo_ref[...], preferred_element_type=F32)
    xo_ref[0] = xo
    h2 = _rms(xo) * n2g_ref[...] * (1.0 + sc_ref[0]) + sh_ref[0]
    hi, lo = _split_bf16(h2)
    h2_ref[0] = hi
    lg_ref[0] = (jnp.dot(hi, rwh_ref[...], preferred_element_type=F32)
                 + jnp.dot(lo, rwh_ref[...], preferred_element_type=F32)
                 + jnp.dot(hi, rwl_ref[...], preferred_element_type=F32)
                 + rb_ref[...])


def _merge(mixed, o, ga, gb, x, g1, sc, sh, wa, wb, wo, n2g, rwh, rwl, rb, *, tm):
    nb, s, d = x.shape
    r = g1.shape[1]
    row = lambda b, i: (b, i, 0)
    mod = (lambda b, i: (b, i, 0)) if r == s else (lambda b, i: (b, 0, 0))
    rblk = tm if r == s else 1
    tok = pl.BlockSpec((1, tm, d), row)
    mods = pl.BlockSpec((1, rblk, d), mod)
    return pl.pallas_call(
        _merge_kernel,
        grid=(nb, s // tm),
        in_specs=[tok, tok, tok, tok, tok, mods, mods, mods,
                  _const_spec((d, d)), _const_spec((d, d)), _const_spec((d, d)),
                  _const_spec((1, d)), _const_spec((d, LANES)), _const_spec((d, LANES)),
                  _const_spec((1, LANES))],
        out_specs=[tok, tok, pl.BlockSpec((1, tm, LANES), row)],
        out_shape=[jax.ShapeDtypeStruct((nb, s, d), F32),
                   jax.ShapeDtypeStruct((nb, s, d), BF16),
                   jax.ShapeDtypeStruct((nb, s, LANES), F32)],
        compiler_params=pltpu.CompilerParams(dimension_semantics=("parallel", "parallel"),
                                             vmem_limit_bytes=VMEM_LIMIT),
        name="merge",
    )(mixed, o, ga, gb, x, g1, sc, sh, wa, wb, wo, n2g, rwh, rwl, rb)


def _w1prep_kernel(w_ref, perm_ref, g_ref, l_ref):
    z = jnp.dot(w_ref[0, 0].astype(BF16), perm_ref[...], preferred_element_type=F32)
    g_ref[0] = z[:, :LANES].astype(BF16)
    l_ref[0] = z[:, LANES:].astype(BF16)


def _w1prep(layer, w1):
    _, n_e, d, f2 = w1.shape
    f = f2 // 2
    src = jnp.concatenate([jnp.arange(0, MXU_DIM, 2), jnp.arange(1, MXU_DIM, 2)])
    perm = (jnp.arange(MXU_DIM)[:, None] == src[None, :]).astype(BF16)
    out = jax.ShapeDtypeStruct((n_e, d, f), BF16)
    return pl.pallas_call(
        _w1prep_kernel,
        grid=(n_e, f2 // MXU_DIM),
        in_specs=[pl.BlockSpec((1, 1, d, MXU_DIM), lambda e, j: (layer, e, 0, j)),
                  _const_spec((MXU_DIM, MXU_DIM))],
        out_specs=[pl.BlockSpec((1, d, LANES), lambda e, j: (e, 0, j)),
                   pl.BlockSpec((1, d, LANES), lambda e, j: (e, 0, j))],
        out_shape=[out, out],
        compiler_params=pltpu.CompilerParams(dimension_semantics=("parallel", "parallel")),
        name="w1prep",
    )(w1, perm)


def _expert_kernel(be_ref, nu_ref, x_ref, w1g_ref, w1l_ref, b1g_ref, b1l_ref, w2_ref, b2_ref, y_ref):
    del be_ref

    @pl.when(pl.program_id(0) < nu_ref[0])
    def _():
        x = x_ref[...]
        glu = jnp.minimum(jnp.dot(x, w1g_ref[0], preferred_element_type=F32) + b1g_ref[0], SWIGLU_LIMIT)
        lin = jnp.clip(jnp.dot(x, w1l_ref[0], preferred_element_type=F32) + b1l_ref[0],
                       -SWIGLU_LIMIT, SWIGLU_LIMIT)
        act = glu * jax.nn.sigmoid(SWIGLU_ALPHA * glu) * (lin + 1.0)
        y_ref[...] = jnp.dot(act.astype(BF16), w2_ref[0], preferred_element_type=F32) + b2_ref[0]

    @pl.when(pl.program_id(0) >= nu_ref[0])
    def _():
        y_ref[...] = jnp.zeros(y_ref.shape, F32)


def _experts(block_expert, n_used, xs, w1g, w1l, b1g, b1l, w2, b2, *, tb):
    n_slots, d = xs.shape
    n_e, _, f = w1g.shape
    emap = lambda i, be, nu: (be[i], 0, 0)
    grid_spec = pltpu.PrefetchScalarGridSpec(
        num_scalar_prefetch=2,
        grid=(n_slots // tb,),
        in_specs=[pl.BlockSpec((tb, d), lambda i, be, nu: (i, 0)),
                  pl.BlockSpec((1, d, f), emap),
                  pl.BlockSpec((1, d, f), emap),
                  pl.BlockSpec((1, 1, f), emap),
                  pl.BlockSpec((1, 1, f), emap),
                  pl.BlockSpec((1, f, d), emap),
                  pl.BlockSpec((1, 1, d), emap)],
        out_specs=pl.BlockSpec((tb, d), lambda i, be, nu: (i, 0)))
    return pl.pallas_call(
        _expert_kernel,
        grid_spec=grid_spec,
        out_shape=jax.ShapeDtypeStruct((n_slots, d), F32),
        compiler_params=pltpu.CompilerParams(dimension_semantics=("arbitrary",),
                                             vmem_limit_bytes=VMEM_LIMIT),
        name="experts",
    )(block_expert, n_used, xs, w1g, w1l, b1g.reshape(n_e, 1, f), b1l.reshape(n_e, 1, f), w2,
      b2.reshape(n_e, 1, d))


def _combine_kernel(yg_ref, gate_ref, x_ref, g2_ref, o_ref):
    gates = gate_ref[0]
    y = yg_ref[0, 0] * gates[:, 0:1]
    for k in range(1, TOP_K):
        y = y + yg_ref[k, 0] * gates[:, k:k + 1]
    o_ref[0] = x_ref[0] + g2_ref[0] * y


def _combine(yg, gates, x, g2, *, tm):
    nb, s, d = x.shape
    r = g2.shape[1]
    row = lambda b, i: (b, i, 0)
    mod = (lambda b, i: (b, i, 0)) if r == s else (lambda b, i: (b, 0, 0))
    rblk = tm if r == s else 1
    return pl.pallas_call(
        _combine_kernel,
        grid=(nb, s // tm),
        in_specs=[pl.BlockSpec((TOP_K, 1, tm, d), lambda b, i: (0, b, i, 0)),
                  pl.BlockSpec((1, tm, TOP_K), row),
                  pl.BlockSpec((1, tm, d), row),
                  pl.BlockSpec((1, rblk, d), mod)],
        out_specs=pl.BlockSpec((1, tm, d), row),
        out_shape=jax.ShapeDtypeStruct((nb, s, d), F32),
        compiler_params=pltpu.CompilerParams(dimension_semantics=("parallel", "parallel"),
                                             vmem_limit_bytes=VMEM_LIMIT),
        name="combine",
    )(yg, gates, x, g2)


def _moe(h2, logits, x, g2, w1g, w1l, b1g, b1l, w2, b2, *, tb, tm):
    nb, s, d = h2.shape
    t = nb * s
    n_assign = t * TOP_K
    top_val, top_idx = lax.top_k(logits.reshape(t, LANES)[:, :N_EXPERTS], TOP_K)
    gates = jax.nn.softmax(top_val, axis=-1)
    e_flat = top_idx.reshape(n_assign)
    onehot = (e_flat[:, None] == jnp.arange(N_EXPERTS, dtype=jnp.int32)[None, :]).astype(jnp.int32)
    csum = jnp.cumsum(onehot, axis=0)
    rank = jnp.sum(onehot * csum, axis=1) - 1
    counts = csum[-1]
    padded = (counts + tb - 1) // tb * tb
    pad_ends = jnp.cumsum(padded)
    pad_starts = pad_ends - padded
    dest = pad_starts[e_flat] + rank
    n_blocks = -(-n_assign // tb) + N_EXPERTS
    n_slots = n_blocks * tb
    tok_flat = jnp.repeat(jnp.arange(t, dtype=jnp.int32), TOP_K)
    slot_tok = jnp.full((n_slots,), t, jnp.int32).at[dest].set(tok_flat)
    block_start = jnp.arange(n_blocks, dtype=jnp.int32) * tb
    block_expert = jnp.minimum(jnp.sum((block_start[:, None] >= pad_ends[None, :]).astype(jnp.int32), axis=1),
                               N_EXPERTS - 1)
    n_used = (pad_ends[-1:] // tb).astype(jnp.int32)
    h_pad = jnp.concatenate([h2.reshape(t, d), jnp.zeros((1, d), h2.dtype)], axis=0)
    xs = h_pad[slot_tok]
    yb = _experts(block_expert, n_used, xs, w1g, w1l, b1g, b1l, w2, b2, tb=tb)
    yg = yb[dest.reshape(t, TOP_K).T].reshape(TOP_K, nb, s, d)
    return _combine(yg, gates.reshape(nb, s, TOP_K), x, g2, tm=tm)


def _lambda_init(layer):
    return 0.8 - 0.6 * math.exp(-0.3 * layer)


def _decode_bias(n_pos, visible):
    col = jnp.arange(n_pos * KV_HEADS, dtype=jnp.int32)[None, :]
    row = jnp.arange(2 * N_HEADS, dtype=jnp.int32)[:, None]
    ok = ((col // n_pos) == ((row % N_HEADS) // GQA_GROUP)) & ((col % n_pos) < visible)
    return jnp.where(ok, 0.0, NEG).astype(F32)


def kernel(x_prompt, x_sample, c_prompt, c_sample, cache_k, cache_v, page_table, ada_w, ada_b, norm1_g, norm2_g, w_in, q_norm_g, k_norm_g, lambda_q1, lambda_k1, lambda_q2, lambda_k2, subln_g, ln_v_g, ln_v_b, w_spatial, b_spatial, w_branch_a, w_branch_b, w_out, router_w, router_b, w1, b1, w2, b2):
    depth = ada_w.shape[0]
    nb, seq, d = x_prompt.shape
    db, ds, _ = x_sample.shape
    assert ds == 1, "sampled tokens open a fresh chunk one row at a time"
    n_phys, page = cache_k.shape[1], cache_k.shape[2]

    mods = _ada_mods(jnp.concatenate([c_prompt, c_sample], axis=0), ada_w, ada_b)
    ck = jnp.transpose(cache_k, (0, 1, 3, 4, 5, 2))
    cv = cache_v.reshape(depth, n_phys, page * KV_HEADS, V_DIM)
    bias = _decode_bias(page, page)
    bias_new = _decode_bias(page, ds)
    tri = jnp.tril(jnp.ones((CHUNK, CHUNK), F32))

    yp = x_prompt
    ys = x_sample.reshape(1, db, d)
    kp_l, vp_l, ks_l, vs_l, cv_l = [], [], [], [], []
    for l in range(depth):
        lam_init = _lambda_init(l)
        lam = (jnp.exp(jnp.sum(lambda_q1[l] * lambda_k1[l])) - jnp.exp(jnp.sum(lambda_q2[l] * lambda_k2[l]))
               + lam_init).reshape(1).astype(F32)
        m6 = mods[l].reshape(nb + db, 6, d)
        mp = [m6[:nb, i][:, None, :] for i in range(6)]
        msm = [m6[nb:, i][None, :, :] for i in range(6)]
        w_in_b = w_in[l].astype(BF16)
        n1g = norm1_g[l][None, :]
        n2g = norm2_g[l][None, :]
        lng, lnb = ln_v_g[l][None, :], ln_v_b[l][None, :]
        qg = jnp.tile(q_norm_g[l], 2)[None, :]
        kg = jnp.tile(k_norm_g[l], 2)[None, :]
        sg = subln_g[l][None, :]
        sg_col = subln_g[l][:, None]
        ws_p = (w_spatial[l] * tri).astype(BF16)
        bs_p = jnp.repeat(b_spatial[l].T, LANES, axis=1)
        ws_s = jnp.repeat(w_spatial[l][:, 0, 0], LANES)[None, :]
        bs_s = jnp.repeat(b_spatial[l][:, 0], LANES)[None, :]
        wa, wb, wo = (w_branch_a[l].astype(BF16), w_branch_b[l].astype(BF16), w_out[l].astype(BF16))
        rw = jnp.pad(router_w[l], ((0, 0), (0, LANES - N_EXPERTS)))
        rwh, rwl = _split_bf16(rw)
        rb = jnp.pad(router_b[l], (0, LANES - N_EXPERTS))[None, :]
        w1g, w1l = _w1prep(l, w1)
        b1g, b1l = b1[l][:, 0::2], b1[l][:, 1::2]
        w2p = w2[l].astype(BF16)
        ew = (w1g, w1l, b1g, b1l, w2p, b2[l])
        out_scale = 1.0 - lam_init

        mixed, q, k, v, kb, vb, ga, gb = _inproj(yp, mp[1], mp[0], n1g, w_in_b, lng, lnb, qg, kg,
                                                 ws_p, bs_p, sample=False, tm=TM_DENSE)
        o = _attn_prompt(lam, q, kb, vb, sg_col, out_scale=out_scale, tq=TQ, tk=TK)
        xo, h2, lg = _merge(mixed, o, ga, gb, yp, mp[2], mp[4], mp[3], wa, wb, wo, n2g, rwh, rwl, rb,
                            tm=TM_DENSE)
        yp = _moe(h2, lg, xo, mp[5], *ew, tb=TB_PROMPT, tm=TM_DENSE)
        kp_l.append(k.reshape(nb, seq, KV_HEADS, 2, HEAD_DIM))
        vp_l.append(v.reshape(nb, seq, KV_HEADS, V_DIM))

        mixed, q, k, v, kb, vb, ga, gb, cvs = _inproj(ys, msm[1], msm[0], n1g, w_in_b, lng, lnb, qg, kg,
                                                      ws_s, bs_s, sample=True, tm=db)
        q16 = q.reshape(db, N_HEADS, 2, HEAD_DIM).transpose(0, 2, 1, 3).reshape(db, 2 * N_HEADS, HEAD_DIM)
        k_new = jnp.pad(k.reshape(db, KV_HEADS, 2, HEAD_DIM, 1), ((0, 0),) * 4 + ((0, page - 1),))
        v_new = jnp.pad(v.reshape(db, KV_HEADS, V_DIM), ((0, 0), (0, (page - 1) * KV_HEADS), (0, 0)))
        o = _attn_decode(l, page_table, lam, q16, bias, bias_new, k_new, v_new, sg, ck, cv,
                         out_scale=out_scale)
        o = o.reshape(1, db, d).astype(BF16)
        xo, h2, lg = _merge(mixed, o, ga, gb, ys, msm[2], msm[4], msm[3], wa, wb, wo, n2g, rwh, rwl, rb,
                            tm=db)
        ys = _moe(h2, lg, xo, msm[5], *ew, tb=TB_SAMPLE, tm=db)
        ks_l.append(k.reshape(db, ds, KV_HEADS, 2, HEAD_DIM))
        vs_l.append(v.reshape(db, ds, KV_HEADS, V_DIM))
        cv_l.append(cvs.reshape(db, ds, d))

    return (yp, ys.reshape(db, ds, d), jnp.stack(kp_l), jnp.stack(vp_l), jnp.stack(ks_l),
            jnp.stack(vs_l), jnp.stack(cv_l))
```

```python
import functools
import math

import jax
import jax.numpy as jnp
from jax import lax
from jax.experimental import pallas as pl
from jax.experimental.pallas import tpu as pltpu

F32 = jnp.float32
BF16 = jnp.bfloat16

LANES = 128
MXU_DIM = 256
HEAD_DIM = 64
N_HEADS = 8
KV_HEADS = 4
GQA_GROUP = N_HEADS // KV_HEADS
V_DIM = 2 * HEAD_DIM
CHUNK = 128
A_GROUPS = 8
N_EXPERTS = 32
TOP_K = 4
SWIGLU_ALPHA = 1.702
SWIGLU_LIMIT = 7.0
LOG2E = math.log2(math.e)
NEG = -1e30

TM_DENSE = 256
TQ = 256
TK = 512
ONES_ROWS = 16
PAGES_PER_STEP = 16
TB_PROMPT = 256
TB_SAMPLE = 16
VMEM_LIMIT = 48 * 1024 * 1024


def _const_spec(shape):
    nd = len(shape)
    return pl.BlockSpec(shape, lambda *_: (0,) * nd, pipeline_mode=pl.Buffered(1))


def _gelu(x):
    return x * (lax.erf(x * (1.0 / math.sqrt(2.0))) + 1.0) * 0.5


def _rms(x, eps=1e-6):
    return x * lax.rsqrt(jnp.mean(x * x, axis=-1, keepdims=True) + eps)


def _ada_kernel(c_ref, w_ref, b_ref, o_ref):
    c = c_ref[...]
    sc = (c * jax.nn.sigmoid(c)).astype(BF16)
    o_ref[0] = jnp.dot(sc, w_ref[0].astype(BF16), preferred_element_type=F32) + b_ref[0]


def _ada_mods(c_all, ada_w, ada_b):
    depth, d, n = ada_w.shape
    r = c_all.shape[0]
    tn = 512
    return pl.pallas_call(
        _ada_kernel,
        grid=(depth, n // tn),
        in_specs=[pl.BlockSpec((r, d), lambda l, j: (0, 0)),
                  pl.BlockSpec((1, d, tn), lambda l, j: (l, 0, j)),
                  pl.BlockSpec((1, 1, tn), lambda l, j: (l, 0, j))],
        out_specs=pl.BlockSpec((1, r, tn), lambda l, j: (l, 0, j)),
        out_shape=jax.ShapeDtypeStruct((depth, r, n), F32),
        compiler_params=pltpu.CompilerParams(dimension_semantics=("parallel", "parallel")),
        name="ada",
    )(c_all, ada_w, ada_b.reshape(depth, 1, n))


def _half_rms(t, g_row, first):
    s = t * t
    s1 = jnp.sum(jnp.where(first, s, 0.0), axis=-1, keepdims=True)
    s2 = jnp.sum(jnp.where(first, 0.0, s), axis=-1, keepdims=True)
    r = jnp.where(first, lax.rsqrt(s1 * (1.0 / HEAD_DIM) + 1e-6),
                  lax.rsqrt(s2 * (1.0 / HEAD_DIM) + 1e-6))
    return t * r * g_row


def _inproj_kernel(x_ref, sc_ref, sh_ref, n1g_ref, w_ref, lng_ref, lnb_ref, qg_ref, kg_ref,
                   ws_ref, bs_ref, mixed_ref, q_ref, k_ref, v_ref, kb_ref, vb_ref, ga_ref, gb_ref,
                   *cv_ref, sample, q_scale):
    d = x_ref.shape[-1]
    tm = x_ref.shape[1]
    x = x_ref[0]
    h = (_rms(x) * n1g_ref[...] * (1.0 + sc_ref[0]) + sh_ref[0]).astype(BF16)

    def proj(lo, width):
        return jnp.dot(h, w_ref[:, lo:lo + width], preferred_element_type=F32)

    u = _gelu(proj(0, d))
    va = _gelu(proj(d, d))
    dv = va - jnp.mean(va, axis=-1, keepdims=True)
    van = dv * lax.rsqrt(jnp.mean(dv * dv, axis=-1, keepdims=True) + 1e-5) * lng_ref[...] + lnb_ref[...]
    if sample:
        cv_ref[0][0] = van
        mixed_ref[0] = (u * (van * ws_ref[...] + bs_ref[...])).astype(BF16)
    else:
        vb16 = van.astype(BF16)
        for c in range(tm // CHUNK):
            rs = slice(c * CHUNK, (c + 1) * CHUNK)
            for g in range(A_GROUPS):
                cs = slice(g * LANES, (g + 1) * LANES)
                mix = jnp.dot(ws_ref[g], vb16[rs, cs], preferred_element_type=F32) + bs_ref[:, cs]
                mixed_ref[0, rs, cs] = (u[rs, cs] * mix).astype(BF16)

    first = lax.broadcasted_iota(jnp.int32, (1, LANES), 1) < HEAD_DIM
    zq = proj(2 * d, N_HEADS * LANES)
    for hh in range(N_HEADS):
        cs = slice(hh * LANES, (hh + 1) * LANES)
        q_ref[0, :, cs] = (_half_rms(zq[:, cs], qg_ref[...], first) * q_scale).astype(BF16)
    zk = proj(2 * d + N_HEADS * LANES, KV_HEADS * LANES)
    for hh in range(KV_HEADS):
        cs = slice(hh * LANES, (hh + 1) * LANES)
        kn = _half_rms(zk[:, cs], kg_ref[...], first)
        k_ref[0, :, cs] = kn
        kb_ref[0, :, cs] = kn.astype(BF16)
    zv = proj(2 * d + (N_HEADS + KV_HEADS) * LANES, KV_HEADS * LANES)
    v_ref[0] = zv
    vb_ref[0] = zv.astype(BF16)
    base = 2 * d + (N_HEADS + 2 * KV_HEADS) * LANES
    ga_ref[0] = jax.nn.sigmoid(proj(base, d)).astype(BF16)
    gb_ref[0] = jax.nn.sigmoid(proj(base + d, d)).astype(BF16)


def _inproj(x, sc, sh, n1g, w_in, lng, lnb, qg, kg, ws, bs, *, sample, tm, q_scale):
    nb, s, d = x.shape
    r = sc.shape[1]
    kw = KV_HEADS * LANES
    row = lambda b, i: (b, i, 0)
    mod = (lambda b, i: (b, i, 0)) if r == s else (lambda b, i: (b, 0, 0))
    rblk = tm if r == s else 1
    in_specs = [pl.BlockSpec((1, tm, d), row),
                pl.BlockSpec((1, rblk, d), mod),
                pl.BlockSpec((1, rblk, d), mod),
                _const_spec((1, d)),
                _const_spec(w_in.shape),
                _const_spec((1, d)), _const_spec((1, d)),
                _const_spec((1, LANES)), _const_spec((1, LANES)),
                _const_spec(ws.shape), _const_spec(bs.shape)]
    out_shapes = [jax.ShapeDtypeStruct((nb, s, d), BF16),
                  jax.ShapeDtypeStruct((nb, s, d), BF16),
                  jax.ShapeDtypeStruct((nb, s, kw), F32),
                  jax.ShapeDtypeStruct((nb, s, kw), F32),
                  jax.ShapeDtypeStruct((nb, s, kw), BF16),
                  jax.ShapeDtypeStruct((nb, s, kw), BF16),
                  jax.ShapeDtypeStruct((nb, s, d), BF16),
                  jax.ShapeDtypeStruct((nb, s, d), BF16)]
    if sample:
        out_shapes.append(jax.ShapeDtypeStruct((nb, s, d), F32))
    out_specs = [pl.BlockSpec((1, tm, o.shape[-1]), row) for o in out_shapes]
    return pl.pallas_call(
        functools.partial(_inproj_kernel, sample=sample, q_scale=q_scale),
        grid=(nb, s // tm),
        in_specs=in_specs,
        out_specs=out_specs,
        out_shape=out_shapes,
        compiler_params=pltpu.CompilerParams(dimension_semantics=("parallel", "parallel"),
                                             vmem_limit_bytes=VMEM_LIMIT),
        name="inproj_sample" if sample else "inproj",
    )(x, sc, sh, n1g, w_in, lng, lnb, qg, kg, ws, bs)


def _attn_kernel(lam_ref, q_ref, k_ref, v_ref, sg_ref, o_ref, qs_ref, vt_ref, sa_ref, sb_ref, m_ref, acc_ref,
                 *, tq, tk, out_scale):
    qi = pl.program_id(2)
    n_kt = vt_ref.shape[0]

    @pl.when(qi == 0)
    def _():
        def fill(j, carry):
            start = pl.multiple_of(j * tk, tk)
            vt_ref[j, :LANES, :] = v_ref[0, pl.ds(start, tk), :].astype(F32).T.astype(BF16)
            vt_ref[j, LANES:, :] = jnp.ones((ONES_ROWS, tk), BF16)
            return carry
        lax.fori_loop(0, n_kt, fill, 0)

    first = lax.broadcasted_iota(jnp.int32, (LANES, 1), 0) < HEAD_DIM
    for g in range(GQA_GROUP):
        qt = q_ref[0, :, g * LANES:(g + 1) * LANES].astype(F32).T
        qs_ref[:, (2 * g) * tq:(2 * g + 1) * tq] = jnp.where(first, qt, 0.0).astype(BF16)
        qs_ref[:, (2 * g + 1) * tq:(2 * g + 2) * tq] = jnp.where(first, 0.0, qt).astype(BF16)
    m_ref[...] = jnp.full(m_ref.shape, NEG, F32)
    acc_ref[...] = jnp.zeros(acc_ref.shape, F32)

    def qk_into(s_ref, kj):
        start = pl.multiple_of(kj * tk, tk)
        s_ref[...] = jnp.dot(k_ref[0, pl.ds(start, tk), :], qs_ref[...], preferred_element_type=F32)

    def process(s_ref, kj, masked):
        s = s_ref[...]
        if masked:
            kpos = kj * tk + lax.broadcasted_iota(jnp.int32, s.shape, 0)
            qpos = qi * tq + (lax.broadcasted_iota(jnp.int32, s.shape, 1) & (tq - 1))
            s = jnp.where(kpos <= qpos, s, NEG)
        m_prev = m_ref[...]
        m_new = jnp.maximum(m_prev, jnp.max(s, axis=0, keepdims=True))
        alpha = jnp.exp2(m_prev - m_new)
        p = jnp.exp2(s - m_new)
        acc_ref[...] = alpha * acc_ref[...] + jnp.dot(vt_ref[kj], p.astype(BF16),
                                                      preferred_element_type=F32)
        m_ref[...] = m_new

    n_full = (qi * tq) // tk
    qk_into(sa_ref, 0)

    def pair(i, carry):
        qk_into(sb_ref, 2 * i + 1)
        process(sa_ref, 2 * i, False)
        qk_into(sa_ref, 2 * i + 2)
        process(sb_ref, 2 * i + 1, False)
        return carry

    lax.fori_loop(0, n_full // 2, pair, 0)

    @pl.when(n_full % 2 == 1)
    def _():
        qk_into(sb_ref, n_full)
        process(sa_ref, n_full - 1, False)
        process(sb_ref, n_full, True)

    @pl.when(n_full % 2 == 0)
    def _():
        process(sa_ref, n_full, True)

    o_all = acc_ref[:LANES, :] / acc_ref[LANES:LANES + 1, :]
    lam = lam_ref[0]
    for g in range(GQA_GROUP):
        og = o_all[:, (2 * g) * tq:(2 * g + 1) * tq] - lam * o_all[:, (2 * g + 1) * tq:(2 * g + 2) * tq]
        og = og * lax.rsqrt(jnp.mean(og * og, axis=0, keepdims=True) + 1e-6) * sg_ref[...] * out_scale
        o_ref[0, :, g * LANES:(g + 1) * LANES] = og.T.astype(BF16)


def _attn_prompt(lam, q, kb, vb, sg_col, *, out_scale, tq, tk):
    nb, s, d = q.shape
    gw = GQA_GROUP * LANES
    m = 2 * GQA_GROUP * tq
    return pl.pallas_call(
        functools.partial(_attn_kernel, tq=tq, tk=tk, out_scale=out_scale),
        grid=(nb, KV_HEADS, s // tq),
        in_specs=[pl.BlockSpec(memory_space=pltpu.SMEM),
                  pl.BlockSpec((1, tq, gw), lambda b, h, i: (b, i, h)),
                  pl.BlockSpec((1, s, LANES), lambda b, h, i: (b, 0, h)),
                  pl.BlockSpec((1, s, LANES), lambda b, h, i: (b, 0, h)),
                  _const_spec((LANES, 1))],
        out_specs=pl.BlockSpec((1, tq, gw), lambda b, h, i: (b, i, h)),
        out_shape=jax.ShapeDtypeStruct((nb, s, d), BF16),
        scratch_shapes=[pltpu.VMEM((LANES, m), BF16),
                        pltpu.VMEM((s // tk, LANES + ONES_ROWS, tk), BF16),
                        pltpu.VMEM((tk, m), F32),
                        pltpu.VMEM((tk, m), F32),
                        pltpu.VMEM((1, m), F32),
                        pltpu.VMEM((LANES + ONES_ROWS, m), F32)],
        compiler_params=pltpu.CompilerParams(
            dimension_semantics=("parallel", "parallel", "arbitrary"),
            vmem_limit_bytes=VMEM_LIMIT),
        name="attn_prompt",
    )(lam, q, kb, vb, sg_col)


def _decode_kernel(pt_ref, lam_ref, q_ref, bias_ref, biasn_ref, kn_ref, vn_ref, sg_ref, *rest,
                   n_pg, out_scale):
    k_refs = rest[:n_pg]
    v_refs = rest[n_pg:2 * n_pg]
    o_ref = rest[2 * n_pg]
    m_ref, l_ref, acc_ref = rest[2 * n_pg + 1:]
    del pt_ref
    j = pl.program_id(1)
    half = N_HEADS

    @pl.when(j == 0)
    def _():
        m_ref[...] = jnp.full(m_ref.shape, NEG, F32)
        l_ref[...] = jnp.zeros(l_ref.shape, F32)
        acc_ref[...] = jnp.zeros(acc_ref.shape, F32)

    q16 = q_ref[0]

    def scores(kref, idx, bias):
        parts = []
        for mp in range(2):
            kcat = jnp.concatenate([kref[idx + (kvh, mp)] for kvh in range(KV_HEADS)], axis=1)
            sm = jnp.dot(q16, kcat.astype(BF16), preferred_element_type=F32)
            parts.append(sm[mp * half:(mp + 1) * half])
        return jnp.concatenate(parts, axis=0) + bias

    def values(vref, idx):
        n = vref.shape[-2] // KV_HEADS
        return jnp.concatenate([vref[idx + (pl.ds(kvh, n, stride=KV_HEADS), slice(None))]
                                for kvh in range(KV_HEADS)], axis=0).astype(BF16)

    def update(s_list, v_list):
        s = jnp.concatenate(s_list, axis=1) if len(s_list) > 1 else s_list[0]
        m_prev = m_ref[...]
        m_new = jnp.maximum(m_prev, jnp.max(s, axis=-1, keepdims=True))
        alpha = jnp.exp(m_prev - m_new)
        p = jnp.exp(s - m_new)
        l_ref[...] = alpha * l_ref[...] + jnp.sum(p, axis=-1, keepdims=True)
        w = s_list[0].shape[1]
        pv = None
        for i, vv in enumerate(v_list):
            t = jnp.dot(p[:, i * w:(i + 1) * w].astype(BF16), vv, preferred_element_type=F32)
            pv = t if pv is None else pv + t
        acc_ref[...] = alpha * acc_ref[...] + pv
        m_ref[...] = m_new

    bias = bias_ref[...]
    update([scores(kr, (0, 0), bias) for kr in k_refs], [values(vr, (0, 0)) for vr in v_refs])

    @pl.when(j == pl.num_programs(1) - 1)
    def _():
        update([scores(kn_ref, (0,), biasn_ref[...])], [values(vn_ref, (0,))])
        o16 = acc_ref[...] / l_ref[...]
        og = o16[:half] - lam_ref[0] * o16[half:]
        o_ref[0] = _rms(og) * sg_ref[...] * out_scale


def _attn_decode(layer, page_table, lam, q16, bias, bias_new, k_new, v_new, sg, ck, cv, *, out_scale):
    db, n_pages = page_table.shape
    n_pg = PAGES_PER_STEP
    page = ck.shape[-1]
    rows_v = cv.shape[2]
    pt_flat = page_table.reshape(-1)

    def kmap(i):
        return lambda b, j, pt: (layer, pt[b * n_pages + j * n_pg + i], 0, 0, 0, 0)

    def vmap(i):
        return lambda b, j, pt: (layer, pt[b * n_pages + j * n_pg + i], 0, 0)

    per_b = lambda b, j, pt: (b, 0, 0)
    const2 = lambda b, j, pt: (0, 0)
    kblk = (KV_HEADS, 2, HEAD_DIM, page)
    in_specs = [pl.BlockSpec(memory_space=pltpu.SMEM),
                pl.BlockSpec((1, 2 * N_HEADS, HEAD_DIM), per_b),
                pl.BlockSpec(bias.shape, const2),
                pl.BlockSpec(bias_new.shape, const2),
                pl.BlockSpec((1,) + kblk, lambda b, j, pt: (b, 0, 0, 0, 0)),
                pl.BlockSpec((1, rows_v, V_DIM), per_b),
                pl.BlockSpec((1, LANES), const2)]
    in_specs += [pl.BlockSpec((1, 1) + kblk, kmap(i)) for i in range(n_pg)]
    in_specs += [pl.BlockSpec((1, 1, rows_v, V_DIM), vmap(i)) for i in range(n_pg)]
    grid_spec = pltpu.PrefetchScalarGridSpec(
        num_scalar_prefetch=1,
        grid=(db, n_pages // n_pg),
        in_specs=in_specs,
        out_specs=pl.BlockSpec((1, N_HEADS, V_DIM), per_b),
        scratch_shapes=[pltpu.VMEM((2 * N_HEADS, 1), F32),
                        pltpu.VMEM((2 * N_HEADS, 1), F32),
                        pltpu.VMEM((2 * N_HEADS, V_DIM), F32)])
    return pl.pallas_call(
        functools.partial(_decode_kernel, n_pg=n_pg, out_scale=out_scale),
        grid_spec=grid_spec,
        out_shape=jax.ShapeDtypeStruct((db, N_HEADS, V_DIM), F32),
        compiler_params=pltpu.CompilerParams(dimension_semantics=("parallel", "arbitrary"),
                                             vmem_limit_bytes=VMEM_LIMIT),
        name="attn_decode",
    )(pt_flat, lam, q16, bias, bias_new, k_new, v_new, sg, *([ck] * n_pg), *([cv] * n_pg))


def _split_bf16(x):
    hi = x.astype(BF16)
    return hi, (x - hi.astype(F32)).astype(BF16)


def _merge_kernel(mixed_ref, o_ref, ga_ref, gb_ref, x_ref, g1_ref, sc_ref, sh_ref, wa_ref, wb_ref,
                  wo_ref, n2g_ref, rwh_ref, rwl_ref, rb_ref, xo_ref, h2_ref, lg_ref):
    a = jnp.dot(mixed_ref[0], wa_ref[...], preferred_element_type=F32)
    b = jnp.dot(o_ref[0], wb_ref[...], preferred_element_type=F32)
    merged = ga_ref[0].astype(F32) * a + gb_ref[0].astype(F32) * b
    xo = x_ref[0] + g1_ref[0] * jnp.dot(merged.astype(BF16), wo_ref[...], preferred_element_type=F32)
    xo_ref[0] = xo
    h2 = _rms(xo) * n2g_ref[...] * (1.0 + sc_ref[0]) + sh_ref[0]
    hi, lo = _split_bf16(h2)
    h2_ref[0] = hi
    lg_ref[0] = (jnp.dot(hi, rwh_ref[...], preferred_element_type=F32)
                 + jnp.dot(lo, rwh_ref[...], preferred_element_type=F32)
                 + jnp.dot(hi, rwl_ref[...], preferred_element_type=F32)
                 + rb_ref[...])


def _merge(mixed, o, ga, gb, x, g1, sc, sh, wa, wb, wo, n2g, rwh, rwl, rb, *, tm):
    nb, s, d = x.shape
    r = g1.shape[1]
    row = lambda b, i: (b, i, 0)
    mod = (lambda b, i: (b, i, 0)) if r == s else (lambda b, i: (b, 0, 0))
    rblk = tm if r == s else 1
    tok = pl.BlockSpec((1, tm, d), row)
    mods = pl.BlockSpec((1, rblk, d), mod)
    return pl.pallas_call(
        _merge_kernel,
        grid=(nb, s // tm),
        in_specs=[tok, tok, tok, tok, tok, mods, mods, mods,
                  _const_spec((d, d)), _const_spec((d, d)), _const_spec((d, d)),
                  _const_spec((1, d)), _const_spec((d, LANES)), _const_spec((d, LANES)),
                  _const_spec((1, LANES))],
        out_specs=[tok, tok, pl.BlockSpec((1, tm, LANES), row)],
        out_shape=[jax.ShapeDtypeStruct((nb, s, d), F32),
                   jax.ShapeDtypeStruct((nb, s, d), BF16),
                   jax.ShapeDtypeStruct((nb, s, LANES), F32)],
        compiler_params=pltpu.CompilerParams(dimension_semantics=("parallel", "parallel"),
                                             vmem_limit_bytes=VMEM_LIMIT),
        name="merge",
    )(mixed, o, ga, gb, x, g1, sc, sh, wa, wb, wo, n2g, rwh, rwl, rb)


def _wprep_kernel(w1_ref, w2_ref, perm_ref, g_ref, l_ref, w2o_ref):
    for j in range(w1_ref.shape[-1] // MXU_DIM):
        z = jnp.dot(w1_ref[0, 0, :, j * MXU_DIM:(j + 1) * MXU_DIM].astype(BF16), perm_ref[...],
                    preferred_element_type=F32)
        g_ref[0, :, j * LANES:(j + 1) * LANES] = z[:, :LANES].astype(BF16)
        l_ref[0, :, j * LANES:(j + 1) * LANES] = z[:, LANES:].astype(BF16)
    w2o_ref[0] = w2_ref[0, 0].astype(BF16)


def _wprep(layer, w1, w2):
    _, n_e, d, f2 = w1.shape
    f = f2 // 2
    src = jnp.concatenate([jnp.arange(0, MXU_DIM, 2), jnp.arange(1, MXU_DIM, 2)])
    perm = (jnp.arange(MXU_DIM)[:, None] == src[None, :]).astype(BF16)
    return pl.pallas_call(
        _wprep_kernel,
        grid=(n_e,),
        in_specs=[pl.BlockSpec((1, 1, d, f2), lambda e: (layer, e, 0, 0)),
                  pl.BlockSpec((1, 1, f, d), lambda e: (layer, e, 0, 0)),
                  _const_spec((MXU_DIM, MXU_DIM))],
        out_specs=[pl.BlockSpec((1, d, f), lambda e: (e, 0, 0)),
                   pl.BlockSpec((1, d, f), lambda e: (e, 0, 0)),
                   pl.BlockSpec((1, f, d), lambda e: (e, 0, 0))],
        out_shape=[jax.ShapeDtypeStruct((n_e, d, f), BF16),
                   jax.ShapeDtypeStruct((n_e, d, f), BF16),
                   jax.ShapeDtypeStruct((n_e, f, d), BF16)],
        compiler_params=pltpu.CompilerParams(dimension_semantics=("parallel",),
                                             vmem_limit_bytes=VMEM_LIMIT),
        name="wprep",
    )(w1, w2, perm)


def _expert_kernel(be_ref, nu_ref, x_ref, w1g_ref, w1l_ref, b1g_ref, b1l_ref, w2_ref, b2_ref, y_ref):
    del be_ref

    @pl.when(pl.program_id(0) < nu_ref[0])
    def _():
        x = x_ref[...]
        glu = jnp.minimum(jnp.dot(x, w1g_ref[0], preferred_element_type=F32) + b1g_ref[0], SWIGLU_LIMIT)
        lin = jnp.clip(jnp.dot(x, w1l_ref[0], preferred_element_type=F32) + b1l_ref[0],
                       -SWIGLU_LIMIT, SWIGLU_LIMIT)
        act = glu * jax.nn.sigmoid(SWIGLU_ALPHA * glu) * (lin + 1.0)
        y_ref[...] = jnp.dot(act.astype(BF16), w2_ref[0], preferred_element_type=F32) + b2_ref[0]

    @pl.when(pl.program_id(0) >= nu_ref[0])
    def _():
        y_ref[...] = jnp.zeros(y_ref.shape, F32)


def _experts(block_expert, n_used, xs, w1g, w1l, b1g, b1l, w2, b2, *, tb):
    n_slots, d = xs.shape
    n_e, _, f = w1g.shape
    emap = lambda i, be, nu: (be[i], 0, 0)
    grid_spec = pltpu.PrefetchScalarGridSpec(
        num_scalar_prefetch=2,
        grid=(n_slots // tb,),
        in_specs=[pl.BlockSpec((tb, d), lambda i, be, nu: (i, 0)),
                  pl.BlockSpec((1, d, f), emap),
                  pl.BlockSpec((1, d, f), emap),
                  pl.BlockSpec((1, 1, f), emap),
                  pl.BlockSpec((1, 1, f), emap),
                  pl.BlockSpec((1, f, d), emap),
                  pl.BlockSpec((1, 1, d), emap)],
        out_specs=pl.BlockSpec((tb, d), lambda i, be, nu: (i, 0)))
    return pl.pallas_call(
        _expert_kernel,
        grid_spec=grid_spec,
        out_shape=jax.ShapeDtypeStruct((n_slots, d), F32),
        compiler_params=pltpu.CompilerParams(dimension_semantics=("arbitrary",),
                                             vmem_limit_bytes=VMEM_LIMIT),
        name="experts",
    )(block_expert, n_used, xs, w1g, w1l, b1g.reshape(n_e, 1, f), b1l.reshape(n_e, 1, f), w2,
      b2.reshape(n_e, 1, d))


def _combine_kernel(yg_ref, gate_ref, x_ref, g2_ref, o_ref):
    gates = gate_ref[0]
    y = yg_ref[0, 0] * gates[:, 0:1]
    for k in range(1, TOP_K):
        y = y + yg_ref[k, 0] * gates[:, k:k + 1]
    o_ref[0] = x_ref[0] + g2_ref[0] * y


def _combine(yg, gates, x, g2, *, tm):
    nb, s, d = x.shape
    r = g2.shape[1]
    row = lambda b, i: (b, i, 0)
    mod = (lambda b, i: (b, i, 0)) if r == s else (lambda b, i: (b, 0, 0))
    rblk = tm if r == s else 1
    return pl.pallas_call(
        _combine_kernel,
        grid=(nb, s // tm),
        in_specs=[pl.BlockSpec((TOP_K, 1, tm, d), lambda b, i: (0, b, i, 0)),
                  pl.BlockSpec((1, tm, TOP_K), row),
                  pl.BlockSpec((1, tm, d), row),
                  pl.BlockSpec((1, rblk, d), mod)],
        out_specs=pl.BlockSpec((1, tm, d), row),
        out_shape=jax.ShapeDtypeStruct((nb, s, d), F32),
        compiler_params=pltpu.CompilerParams(dimension_semantics=("parallel", "parallel"),
                                             vmem_limit_bytes=VMEM_LIMIT),
        name="combine",
    )(yg, gates, x, g2)


def _moe(h2, logits, x, g2, w1g, w1l, b1g, b1l, w2, b2, *, tb, tm):
    nb, s, d = h2.shape
    t = nb * s
    n_assign = t * TOP_K
    top_val, top_idx = lax.top_k(logits.reshape(t, LANES)[:, :N_EXPERTS], TOP_K)
    gates = jax.nn.softmax(top_val, axis=-1)
    e_flat = top_idx.reshape(n_assign)
    onehot = (e_flat[:, None] == jnp.arange(N_EXPERTS, dtype=jnp.int32)[None, :]).astype(jnp.int32)
    csum = jnp.cumsum(onehot, axis=0)
    rank = jnp.sum(onehot * csum, axis=1) - 1
    counts = csum[-1]
    padded = (counts + tb - 1) // tb * tb
    pad_ends = jnp.cumsum(padded)
    pad_starts = pad_ends - padded
    dest = pad_starts[e_flat] + rank
    n_blocks = -(-n_assign // tb) + N_EXPERTS
    n_slots = n_blocks * tb
    tok_flat = jnp.repeat(jnp.arange(t, dtype=jnp.int32), TOP_K)
    slot_tok = jnp.full((n_slots,), t, jnp.int32).at[dest].set(tok_flat)
    block_start = jnp.arange(n_blocks, dtype=jnp.int32) * tb
    block_expert = jnp.minimum(jnp.sum((block_start[:, None] >= pad_ends[None, :]).astype(jnp.int32), axis=1),
                               N_EXPERTS - 1)
    n_used = (pad_ends[-1:] // tb).astype(jnp.int32)
    h_pad = jnp.concatenate([h2.reshape(t, d), jnp.zeros((1, d), h2.dtype)], axis=0)
    xs = h_pad[slot_tok]
    yb = _experts(block_expert, n_used, xs, w1g, w1l, b1g, b1l, w2, b2, tb=tb)
    yg = yb[dest.reshape(t, TOP_K).T].reshape(TOP_K, nb, s, d)
    return _combine(yg, gates.reshape(nb, s, TOP_K), x, g2, tm=tm)


def _lambda_init(layer):
    return 0.8 - 0.6 * math.exp(-0.3 * layer)


def _decode_bias(n_pos, visible):
    col = jnp.arange(n_pos * KV_HEADS, dtype=jnp.int32)[None, :]
    row = jnp.arange(2 * N_HEADS, dtype=jnp.int32)[:, None]
    ok = ((col // n_pos) == ((row % N_HEADS) // GQA_GROUP)) & ((col % n_pos) < visible)
    return jnp.where(ok, 0.0, NEG).astype(F32)


def kernel(x_prompt, x_sample, c_prompt, c_sample, cache_k, cache_v, page_table, ada_w, ada_b, norm1_g, norm2_g, w_in, q_norm_g, k_norm_g, lambda_q1, lambda_k1, lambda_q2, lambda_k2, subln_g, ln_v_g, ln_v_b, w_spatial, b_spatial, w_branch_a, w_branch_b, w_out, router_w, router_b, w1, b1, w2, b2):
    depth = ada_w.shape[0]
    nb, seq, d = x_prompt.shape
    db, ds, _ = x_sample.shape
    assert ds == 1, "sampled tokens open a fresh chunk one row at a time"
    n_phys, page = cache_k.shape[1], cache_k.shape[2]

    mods = _ada_mods(jnp.concatenate([c_prompt, c_sample], axis=0), ada_w, ada_b)
    ck = jnp.transpose(cache_k, (0, 1, 3, 4, 5, 2))
    cv = cache_v.reshape(depth, n_phys, page * KV_HEADS, V_DIM)
    bias = _decode_bias(page, page)
    bias_new = _decode_bias(page, ds)
    tri = jnp.tril(jnp.ones((CHUNK, CHUNK), F32))

    yp = x_prompt
    ys = x_sample.reshape(1, db, d)
    kp_l, vp_l, ks_l, vs_l, cv_l = [], [], [], [], []
    for l in range(depth):
        lam_init = _lambda_init(l)
        lam = (jnp.exp(jnp.sum(lambda_q1[l] * lambda_k1[l])) - jnp.exp(jnp.sum(lambda_q2[l] * lambda_k2[l]))
               + lam_init).reshape(1).astype(F32)
        m6 = mods[l].reshape(nb + db, 6, d)
        mp = [m6[:nb, i][:, None, :] for i in range(6)]
        msm = [m6[nb:, i][None, :, :] for i in range(6)]
        w_in_b = w_in[l].astype(BF16)
        n1g = norm1_g[l][None, :]
        n2g = norm2_g[l][None, :]
        lng, lnb = ln_v_g[l][None, :], ln_v_b[l][None, :]
        qg = jnp.tile(q_norm_g[l], 2)[None, :]
        kg = jnp.tile(k_norm_g[l], 2)[None, :]
        sg = subln_g[l][None, :]
        sg_col = subln_g[l][:, None]
        ws_p = (w_spatial[l] * tri).astype(BF16)
        bs_p = jnp.repeat(b_spatial[l].T, LANES, axis=1)
        ws_s = jnp.repeat(w_spatial[l][:, 0, 0], LANES)[None, :]
        bs_s = jnp.repeat(b_spatial[l][:, 0], LANES)[None, :]
        wa, wb, wo = (w_branch_a[l].astype(BF16), w_branch_b[l].astype(BF16), w_out[l].astype(BF16))
        rw = jnp.pad(router_w[l], ((0, 0), (0, LANES - N_EXPERTS)))
        rwh, rwl = _split_bf16(rw)
        rb = jnp.pad(router_b[l], (0, LANES - N_EXPERTS))[None, :]
        w1g, w1l, w2p = _wprep(l, w1, w2)
        b1g, b1l = b1[l][:, 0::2], b1[l][:, 1::2]
        ew = (w1g, w1l, b1g, b1l, w2p, b2[l])
        out_scale = 1.0 - lam_init

        mixed, q, k, v, kb, vb, ga, gb = _inproj(yp, mp[1], mp[0], n1g, w_in_b, lng, lnb, qg, kg,
                                                 ws_p, bs_p, sample=False, tm=TM_DENSE,
                                                 q_scale=HEAD_DIM ** -0.5 * LOG2E)
        o = _attn_prompt(lam, q, kb, vb, sg_col, out_scale=out_scale, tq=TQ, tk=TK)
        xo, h2, lg = _merge(mixed, o, ga, gb, yp, mp[2], mp[4], mp[3], wa, wb, wo, n2g, rwh, rwl, rb,
                            tm=TM_DENSE)
        yp = _moe(h2, lg, xo, mp[5], *ew, tb=TB_PROMPT, tm=TM_DENSE)
        kp_l.append(k.reshape(nb, seq, KV_HEADS, 2, HEAD_DIM))
        vp_l.append(v.reshape(nb, seq, KV_HEADS, V_DIM))

        mixed, q, k, v, kb, vb, ga, gb, cvs = _inproj(ys, msm[1], msm[0], n1g, w_in_b, lng, lnb, qg, kg,
                                                      ws_s, bs_s, sample=True, tm=db,
                                                      q_scale=HEAD_DIM ** -0.5)
        q16 = q.reshape(db, N_HEADS, 2, HEAD_DIM).transpose(0, 2, 1, 3).reshape(db, 2 * N_HEADS, HEAD_DIM)
        k_new = jnp.pad(k.reshape(db, KV_HEADS, 2, HEAD_DIM, 1), ((0, 0),) * 4 + ((0, page - 1),))
        v_new = jnp.pad(v.reshape(db, KV_HEADS, V_DIM), ((0, 0), (0, (page - 1) * KV_HEADS), (0, 0)))
        o = _attn_decode(l, page_table, lam, q16, bias, bias_new, k_new, v_new, sg, ck, cv,
                         out_scale=out_scale)
        o = o.reshape(1, db, d).astype(BF16)
        xo, h2, lg = _merge(mixed, o, ga, gb, ys, msm[2], msm[4], msm[3], wa, wb, wo, n2g, rwh, rwl, rb,
                            tm=db)
        ys = _moe(h2, lg, xo, msm[5], *ew, tb=TB_SAMPLE, tm=db)
        ks_l.append(k.reshape(db, ds, KV_HEADS, 2, HEAD_DIM))
        vs_l.append(v.reshape(db, ds, KV_HEADS, V_DIM))
        cv_l.append(cvs.reshape(db, ds, d))

    return (yp, ys.reshape(db, ds, d), jnp.stack(kp_l), jnp.stack(vp_l), jnp.stack(ks_l),
            jnp.stack(vs_l), jnp.stack(cv_l))
```

```python
import functools
import math

import jax
import jax.numpy as jnp
from jax import lax
from jax.experimental import pallas as pl
from jax.experimental.pallas import tpu as pltpu

F32 = jnp.float32
BF16 = jnp.bfloat16

LANES = 128
MXU_DIM = 256
HEAD_DIM = 64
N_HEADS = 8
KV_HEADS = 4
GQA_GROUP = N_HEADS // KV_HEADS
V_DIM = 2 * HEAD_DIM
CHUNK = 128
A_GROUPS = 8
N_EXPERTS = 32
TOP_K = 4
SWIGLU_ALPHA = 1.702
SWIGLU_LIMIT = 7.0
LOG2E = math.log2(math.e)
NEG = -1e30

TM_DENSE = 256
TQ = 256
TK = 512
ONES_ROWS = 16
PAGES_PER_STEP = 16
TB_PROMPT = 256
TB_SAMPLE = 16
VMEM_LIMIT = 48 * 1024 * 1024


def _const_spec(shape):
    nd = len(shape)
    return pl.BlockSpec(shape, lambda *_: (0,) * nd, pipeline_mode=pl.Buffered(1))


def _gelu(x):
    return x * (lax.erf(x * (1.0 / math.sqrt(2.0))) + 1.0) * 0.5


def _rms(x, eps=1e-6):
    return x * lax.rsqrt(jnp.mean(x * x, axis=-1, keepdims=True) + eps)


def _ada_kernel(c_ref, w_ref, b_ref, o_ref):
    c = c_ref[...]
    sc = (c * jax.nn.sigmoid(c)).astype(BF16)
    o_ref[0] = jnp.dot(sc, w_ref[0].astype(BF16), preferred_element_type=F32) + b_ref[0]


def _ada_mods(c_all, ada_w, ada_b):
    depth, d, n = ada_w.shape
    r = c_all.shape[0]
    tn = 512
    return pl.pallas_call(
        _ada_kernel,
        grid=(depth, n // tn),
        in_specs=[pl.BlockSpec((r, d), lambda l, j: (0, 0)),
                  pl.BlockSpec((1, d, tn), lambda l, j: (l, 0, j)),
                  pl.BlockSpec((1, 1, tn), lambda l, j: (l, 0, j))],
        out_specs=pl.BlockSpec((1, r, tn), lambda l, j: (l, 0, j)),
        out_shape=jax.ShapeDtypeStruct((depth, r, n), F32),
        compiler_params=pltpu.CompilerParams(dimension_semantics=("parallel", "parallel")),
        name="ada",
    )(c_all, ada_w, ada_b.reshape(depth, 1, n))


def _half_rms(t, g_row, first):
    s = t * t
    s1 = jnp.sum(jnp.where(first, s, 0.0), axis=-1, keepdims=True)
    s2 = jnp.sum(jnp.where(first, 0.0, s), axis=-1, keepdims=True)
    r = jnp.where(first, lax.rsqrt(s1 * (1.0 / HEAD_DIM) + 1e-6),
                  lax.rsqrt(s2 * (1.0 / HEAD_DIM) + 1e-6))
    return t * r * g_row


def _inproj_kernel(x_ref, sc_ref, sh_ref, n1g_ref, w_ref, lng_ref, lnb_ref, qg_ref, kg_ref,
                   ws_ref, bs_ref, mixed_ref, q_ref, k_ref, v_ref, kb_ref, vb_ref, ga_ref, gb_ref,
                   *cv_ref, sample, q_scale):
    d = x_ref.shape[-1]
    tm = x_ref.shape[1]
    x = x_ref[0]
    h = (_rms(x) * n1g_ref[...] * (1.0 + sc_ref[0]) + sh_ref[0]).astype(BF16)

    def proj(lo, width):
        return jnp.dot(h, w_ref[:, lo:lo + width], preferred_element_type=F32)

    u = _gelu(proj(0, d))
    va = _gelu(proj(d, d))
    dv = va - jnp.mean(va, axis=-1, keepdims=True)
    van = dv * lax.rsqrt(jnp.mean(dv * dv, axis=-1, keepdims=True) + 1e-5) * lng_ref[...] + lnb_ref[...]
    if sample:
        cv_ref[0][0] = van
        mixed_ref[0] = (u * (van * ws_ref[...] + bs_ref[...])).astype(BF16)
    else:
        vb16 = van.astype(BF16)
        for c in range(tm // CHUNK):
            rs = slice(c * CHUNK, (c + 1) * CHUNK)
            for g in range(A_GROUPS):
                cs = slice(g * LANES, (g + 1) * LANES)
                mix = jnp.dot(ws_ref[g], vb16[rs, cs], preferred_element_type=F32) + bs_ref[:, cs]
                mixed_ref[0, rs, cs] = (u[rs, cs] * mix).astype(BF16)

    first = lax.broadcasted_iota(jnp.int32, (1, LANES), 1) < HEAD_DIM
    zq = proj(2 * d, N_HEADS * LANES)
    for hh in range(N_HEADS):
        cs = slice(hh * LANES, (hh + 1) * LANES)
        q_ref[0, :, cs] = (_half_rms(zq[:, cs], qg_ref[...], first) * q_scale).astype(BF16)
    zk = proj(2 * d + N_HEADS * LANES, KV_HEADS * LANES)
    for hh in range(KV_HEADS):
        cs = slice(hh * LANES, (hh + 1) * LANES)
        kn = _half_rms(zk[:, cs], kg_ref[...], first)
        k_ref[0, :, cs] = kn
        kb_ref[0, :, cs] = kn.astype(BF16)
    zv = proj(2 * d + (N_HEADS + KV_HEADS) * LANES, KV_HEADS * LANES)
    v_ref[0] = zv
    vb_ref[0] = zv.astype(BF16)
    base = 2 * d + (N_HEADS + 2 * KV_HEADS) * LANES
    ga_ref[0] = jax.nn.sigmoid(proj(base, d)).astype(BF16)
    gb_ref[0] = jax.nn.sigmoid(proj(base + d, d)).astype(BF16)


def _inproj(x, sc, sh, n1g, w_in, lng, lnb, qg, kg, ws, bs, *, sample, tm, q_scale):
    nb, s, d = x.shape
    r = sc.shape[1]
    kw = KV_HEADS * LANES
    row = lambda b, i: (b, i, 0)
    mod = (lambda b, i: (b, i, 0)) if r == s else (lambda b, i: (b, 0, 0))
    rblk = tm if r == s else 1
    in_specs = [pl.BlockSpec((1, tm, d), row),
                pl.BlockSpec((1, rblk, d), mod),
                pl.BlockSpec((1, rblk, d), mod),
                _const_spec((1, d)),
                _const_spec(w_in.shape),
                _const_spec((1, d)), _const_spec((1, d)),
                _const_spec((1, LANES)), _const_spec((1, LANES)),
                _const_spec(ws.shape), _const_spec(bs.shape)]
    out_shapes = [jax.ShapeDtypeStruct((nb, s, d), BF16),
                  jax.ShapeDtypeStruct((nb, s, d), BF16),
                  jax.ShapeDtypeStruct((nb, s, kw), F32),
                  jax.ShapeDtypeStruct((nb, s, kw), F32),
                  jax.ShapeDtypeStruct((nb, s, kw), BF16),
                  jax.ShapeDtypeStruct((nb, s, kw), BF16),
                  jax.ShapeDtypeStruct((nb, s, d), BF16),
                  jax.ShapeDtypeStruct((nb, s, d), BF16)]
    if sample:
        out_shapes.append(jax.ShapeDtypeStruct((nb, s, d), F32))
    out_specs = [pl.BlockSpec((1, tm, o.shape[-1]), row) for o in out_shapes]
    return pl.pallas_call(
        functools.partial(_inproj_kernel, sample=sample, q_scale=q_scale),
        grid=(nb, s // tm),
        in_specs=in_specs,
        out_specs=out_specs,
        out_shape=out_shapes,
        compiler_params=pltpu.CompilerParams(dimension_semantics=("parallel", "parallel"),
                                             vmem_limit_bytes=VMEM_LIMIT),
        name="inproj_sample" if sample else "inproj",
    )(x, sc, sh, n1g, w_in, lng, lnb, qg, kg, ws, bs)


def _attn_kernel(lam_ref, q_ref, k_ref, v_ref, sg_ref, o_ref, qs_ref, vt_ref, sa_ref, sb_ref, m_ref, acc_ref,
                 *, tq, tk, out_scale):
    qi = pl.program_id(2)
    n_kt = vt_ref.shape[0]

    @pl.when(qi == 0)
    def _():
        def fill(j, carry):
            start = pl.multiple_of(j * tk, tk)
            vt_ref[j, :LANES, :] = v_ref[0, pl.ds(start, tk), :].astype(F32).T.astype(BF16)
            vt_ref[j, LANES:, :] = jnp.ones((ONES_ROWS, tk), BF16)
            return carry
        lax.fori_loop(0, n_kt, fill, 0)

    first = lax.broadcasted_iota(jnp.int32, (LANES, 1), 0) < HEAD_DIM
    for g in range(GQA_GROUP):
        qt = q_ref[0, :, g * LANES:(g + 1) * LANES].astype(F32).T
        qs_ref[:, (2 * g) * tq:(2 * g + 1) * tq] = jnp.where(first, qt, 0.0).astype(BF16)
        qs_ref[:, (2 * g + 1) * tq:(2 * g + 2) * tq] = jnp.where(first, 0.0, qt).astype(BF16)
    m_ref[...] = jnp.full(m_ref.shape, NEG, F32)
    acc_ref[...] = jnp.zeros(acc_ref.shape, F32)

    def qk_into(s_ref, kj):
        start = pl.multiple_of(kj * tk, tk)
        s_ref[...] = jnp.dot(k_ref[0, pl.ds(start, tk), :], qs_ref[...], preferred_element_type=F32)

    def process(s_ref, kj, masked):
        s = s_ref[...]
        if masked:
            kpos = kj * tk + lax.broadcasted_iota(jnp.int32, s.shape, 0)
            qpos = qi * tq + (lax.broadcasted_iota(jnp.int32, s.shape, 1) & (tq - 1))
            s = jnp.where(kpos <= qpos, s, NEG)
        m_prev = m_ref[...]
        m_new = jnp.maximum(m_prev, jnp.max(s, axis=0, keepdims=True))
        alpha = jnp.exp2(m_prev - m_new)
        p = jnp.exp2(s - m_new)
        acc_ref[...] = alpha * acc_ref[...] + jnp.dot(vt_ref[kj], p.astype(BF16),
                                                      preferred_element_type=F32)
        m_ref[...] = m_new

    n_full = (qi * tq) // tk
    qk_into(sa_ref, 0)

    def pair(i, carry):
        qk_into(sb_ref, 2 * i + 1)
        process(sa_ref, 2 * i, False)
        qk_into(sa_ref, 2 * i + 2)
        process(sb_ref, 2 * i + 1, False)
        return carry

    lax.fori_loop(0, n_full // 2, pair, 0)

    @pl.when(n_full % 2 == 1)
    def _():
        qk_into(sb_ref, n_full)
        process(sa_ref, n_full - 1, False)
        process(sb_ref, n_full, True)

    @pl.when(n_full % 2 == 0)
    def _():
        process(sa_ref, n_full, True)

    o_all = acc_ref[:LANES, :] / acc_ref[LANES:LANES + 1, :]
    lam = lam_ref[0]
    for g in range(GQA_GROUP):
        og = o_all[:, (2 * g) * tq:(2 * g + 1) * tq] - lam * o_all[:, (2 * g + 1) * tq:(2 * g + 2) * tq]
        og = og * lax.rsqrt(jnp.mean(og * og, axis=0, keepdims=True) + 1e-6) * sg_ref[...] * out_scale
        o_ref[0, :, g * LANES:(g + 1) * LANES] = og.T.astype(BF16)


def _attn_prompt(lam, q, kb, vb, sg_col, *, out_scale, tq, tk):
    nb, s, d = q.shape
    gw = GQA_GROUP * LANES
    m = 2 * GQA_GROUP * tq
    return pl.pallas_call(
        functools.partial(_attn_kernel, tq=tq, tk=tk, out_scale=out_scale),
        grid=(nb, KV_HEADS, s // tq),
        in_specs=[pl.BlockSpec(memory_space=pltpu.SMEM),
                  pl.BlockSpec((1, tq, gw), lambda b, h, i: (b, i, h)),
                  pl.BlockSpec((1, s, LANES), lambda b, h, i: (b, 0, h)),
                  pl.BlockSpec((1, s, LANES), lambda b, h, i: (b, 0, h)),
                  _const_spec((LANES, 1))],
        out_specs=pl.BlockSpec((1, tq, gw), lambda b, h, i: (b, i, h)),
        out_shape=jax.ShapeDtypeStruct((nb, s, d), BF16),
        scratch_shapes=[pltpu.VMEM((LANES, m), BF16),
                        pltpu.VMEM((s // tk, LANES + ONES_ROWS, tk), BF16),
                        pltpu.VMEM((tk, m), F32),
                        pltpu.VMEM((tk, m), F32),
                        pltpu.VMEM((1, m), F32),
                        pltpu.VMEM((LANES + ONES_ROWS, m), F32)],
        compiler_params=pltpu.CompilerParams(
            dimension_semantics=("parallel", "parallel", "arbitrary"),
            vmem_limit_bytes=VMEM_LIMIT),
        name="attn_prompt",
    )(lam, q, kb, vb, sg_col)


def _decode_kernel(pt_ref, lam_ref, q_ref, bias_ref, biasn_ref, kn_ref, vn_ref, sg_ref, *rest,
                   n_pg, out_scale):
    k_refs = rest[:n_pg]
    v_refs = rest[n_pg:2 * n_pg]
    o_ref = rest[2 * n_pg]
    m_ref, l_ref, acc_ref = rest[2 * n_pg + 1:]
    del pt_ref
    j = pl.program_id(1)
    half = N_HEADS

    @pl.when(j == 0)
    def _():
        m_ref[...] = jnp.full(m_ref.shape, NEG, F32)
        l_ref[...] = jnp.zeros(l_ref.shape, F32)
        acc_ref[...] = jnp.zeros(acc_ref.shape, F32)

    q16 = q_ref[0]

    def scores(kref, idx, bias):
        parts = []
        for mp in range(2):
            kcat = jnp.concatenate([kref[idx + (kvh, mp)] for kvh in range(KV_HEADS)], axis=1)
            sm = jnp.dot(q16, kcat.astype(BF16), preferred_element_type=F32)
            parts.append(sm[mp * half:(mp + 1) * half])
        return jnp.concatenate(parts, axis=0) + bias

    def values(vref, idx):
        n = vref.shape[-2] // KV_HEADS
        return jnp.concatenate([vref[idx + (pl.ds(kvh, n, stride=KV_HEADS), slice(None))]
                                for kvh in range(KV_HEADS)], axis=0).astype(BF16)

    def update(s_list, v_list):
        s = jnp.concatenate(s_list, axis=1) if len(s_list) > 1 else s_list[0]
        m_prev = m_ref[...]
        m_new = jnp.maximum(m_prev, jnp.max(s, axis=-1, keepdims=True))
        alpha = jnp.exp(m_prev - m_new)
        p = jnp.exp(s - m_new)
        l_ref[...] = alpha * l_ref[...] + jnp.sum(p, axis=-1, keepdims=True)
        w = s_list[0].shape[1]
        pv = None
        for i, vv in enumerate(v_list):
            t = jnp.dot(p[:, i * w:(i + 1) * w].astype(BF16), vv, preferred_element_type=F32)
            pv = t if pv is None else pv + t
        acc_ref[...] = alpha * acc_ref[...] + pv
        m_ref[...] = m_new

    bias = bias_ref[...]
    update([scores(kr, (0, 0), bias) for kr in k_refs], [values(vr, (0, 0)) for vr in v_refs])

    @pl.when(j == pl.num_programs(1) - 1)
    def _():
        update([scores(kn_ref, (0,), biasn_ref[...])], [values(vn_ref, (0,))])
        o16 = acc_ref[...] / l_ref[...]
        og = o16[:half] - lam_ref[0] * o16[half:]
        o_ref[0] = _rms(og) * sg_ref[...] * out_scale


def _attn_decode(layer, page_table, lam, q16, bias, bias_new, k_new, v_new, sg, ck, cv, *, out_scale):
    db, n_pages = page_table.shape
    n_pg = PAGES_PER_STEP
    page = ck.shape[-1]
    rows_v = cv.shape[2]
    pt_flat = page_table.reshape(-1)

    def kmap(i):
        return lambda b, j, pt: (layer, pt[b * n_pages + j * n_pg + i], 0, 0, 0, 0)

    def vmap(i):
        return lambda b, j, pt: (layer, pt[b * n_pages + j * n_pg + i], 0, 0)

    per_b = lambda b, j, pt: (b, 0, 0)
    const2 = lambda b, j, pt: (0, 0)
    kblk = (KV_HEADS, 2, HEAD_DIM, page)
    in_specs = [pl.BlockSpec(memory_space=pltpu.SMEM),
                pl.BlockSpec((1, 2 * N_HEADS, HEAD_DIM), per_b),
                pl.BlockSpec(bias.shape, const2),
                pl.BlockSpec(bias_new.shape, const2),
                pl.BlockSpec((1,) + kblk, lambda b, j, pt: (b, 0, 0, 0, 0)),
                pl.BlockSpec((1, rows_v, V_DIM), per_b),
                pl.BlockSpec((1, LANES), const2)]
    in_specs += [pl.BlockSpec((1, 1) + kblk, kmap(i)) for i in range(n_pg)]
    in_specs += [pl.BlockSpec((1, 1, rows_v, V_DIM), vmap(i)) for i in range(n_pg)]
    grid_spec = pltpu.PrefetchScalarGridSpec(
        num_scalar_prefetch=1,
        grid=(db, n_pages // n_pg),
        in_specs=in_specs,
        out_specs=pl.BlockSpec((1, N_HEADS, V_DIM), per_b),
        scratch_shapes=[pltpu.VMEM((2 * N_HEADS, 1), F32),
                        pltpu.VMEM((2 * N_HEADS, 1), F32),
                        pltpu.VMEM((2 * N_HEADS, V_DIM), F32)])
    return pl.pallas_call(
        functools.partial(_decode_kernel, n_pg=n_pg, out_scale=out_scale),
        grid_spec=grid_spec,
        out_shape=jax.ShapeDtypeStruct((db, N_HEADS, V_DIM), F32),
        compiler_params=pltpu.CompilerParams(dimension_semantics=("parallel", "arbitrary"),
                                             vmem_limit_bytes=VMEM_LIMIT),
        name="attn_decode",
    )(pt_flat, lam, q16, bias, bias_new, k_new, v_new, sg, *([ck] * n_pg), *([cv] * n_pg))


def _split_bf16(x):
    hi = x.astype(BF16)
    return hi, (x - hi.astype(F32)).astype(BF16)


def _merge_kernel(mixed_ref, o_ref, ga_ref, gb_ref, x_ref, g1_ref, sc_ref, sh_ref, wa_ref, wb_ref,
                  wo_ref, n2g_ref, rwh_ref, rwl_ref, rb_ref, tri_ref,
                  xo_ref, h2_ref, idx_ref, gate_ref, rank_ref, cnt_ref, run_ref):
    @pl.when((pl.program_id(0) == 0) & (pl.program_id(1) == 0))
    def _():
        run_ref[...] = jnp.zeros(run_ref.shape, F32)

    a = jnp.dot(mixed_ref[0], wa_ref[...], preferred_element_type=F32)
    b = jnp.dot(o_ref[0], wb_ref[...], preferred_element_type=F32)
    merged = ga_ref[0].astype(F32) * a + gb_ref[0].astype(F32) * b
    xo = x_ref[0] + g1_ref[0] * jnp.dot(merged.astype(BF16), wo_ref[...], preferred_element_type=F32)
    xo_ref[0] = xo
    h2 = _rms(xo) * n2g_ref[...] * (1.0 + sc_ref[0]) + sh_ref[0]
    hi, lo = _split_bf16(h2)
    h2_ref[0] = hi.astype(F32)
    lg = (jnp.dot(hi, rwh_ref[...], preferred_element_type=F32)
          + jnp.dot(lo, rwh_ref[...], preferred_element_type=F32)
          + jnp.dot(hi, rwl_ref[...], preferred_element_type=F32)
          + rb_ref[...])

    lane = lax.broadcasted_iota(jnp.int32, lg.shape, 1)
    work = lg
    sels, vals, idxs = [], [], []
    for _ in range(TOP_K):
        mx = jnp.max(work, axis=-1, keepdims=True)
        ik = jnp.min(jnp.where(work == mx, lane, LANES), axis=-1, keepdims=True)
        sel = lane == ik
        work = jnp.where(sel, -jnp.inf, work)
        sels.append(sel)
        vals.append(mx)
        idxs.append(ik)
    es = [jnp.exp(v - vals[0]) for v in vals]
    den = es[0]
    for e in es[1:]:
        den = den + e

    onehot = sels[0].astype(F32)
    for sel in sels[1:]:
        onehot = onehot + sel.astype(F32)
    base = jnp.dot(tri_ref[...], onehot.astype(BF16), preferred_element_type=F32) + run_ref[...]
    idx_out = jnp.zeros(lg.shape, jnp.int32)
    gate_out = jnp.zeros(lg.shape, F32)
    rank_out = jnp.zeros(lg.shape, jnp.int32)
    for k in range(TOP_K):
        rk = jnp.sum(jnp.where(sels[k], base, 0.0), axis=-1, keepdims=True)
        idx_out = jnp.where(lane == k, idxs[k], idx_out)
        gate_out = jnp.where(lane == k, es[k] / den, gate_out)
        rank_out = jnp.where(lane == k, rk.astype(jnp.int32), rank_out)
    idx_ref[0] = idx_out
    gate_ref[0] = gate_out
    rank_ref[0] = rank_out
    run_ref[...] = run_ref[...] + jnp.sum(onehot, axis=0, keepdims=True)
    cnt_ref[...] = run_ref[...]


def _merge(mixed, o, ga, gb, x, g1, sc, sh, wa, wb, wo, n2g, rwh, rwl, rb, *, tm):
    nb, s, d = x.shape
    r = g1.shape[1]
    row = lambda b, i: (b, i, 0)
    mod = (lambda b, i: (b, i, 0)) if r == s else (lambda b, i: (b, 0, 0))
    rblk = tm if r == s else 1
    tok = pl.BlockSpec((1, tm, d), row)
    mods = pl.BlockSpec((1, rblk, d), mod)
    route = pl.BlockSpec((1, tm, LANES), row)
    tri = jnp.tril(jnp.ones((tm, tm), F32), -1).astype(BF16)
    return pl.pallas_call(
        _merge_kernel,
        grid=(nb, s // tm),
        in_specs=[tok, tok, tok, tok, tok, mods, mods, mods,
                  _const_spec((d, d)), _const_spec((d, d)), _const_spec((d, d)),
                  _const_spec((1, d)), _const_spec((d, LANES)), _const_spec((d, LANES)),
                  _const_spec((1, LANES)), _const_spec((tm, tm))],
        out_specs=[tok, tok, route, route, route, pl.BlockSpec((1, LANES), lambda b, i: (0, 0))],
        out_shape=[jax.ShapeDtypeStruct((nb, s, d), F32),
                   jax.ShapeDtypeStruct((nb, s, d), F32),
                   jax.ShapeDtypeStruct((nb, s, LANES), jnp.int32),
                   jax.ShapeDtypeStruct((nb, s, LANES), F32),
                   jax.ShapeDtypeStruct((nb, s, LANES), jnp.int32),
                   jax.ShapeDtypeStruct((1, LANES), F32)],
        scratch_shapes=[pltpu.VMEM((1, LANES), F32)],
        compiler_params=pltpu.CompilerParams(dimension_semantics=("arbitrary", "arbitrary"),
                                             vmem_limit_bytes=VMEM_LIMIT),
        name="merge",
    )(mixed, o, ga, gb, x, g1, sc, sh, wa, wb, wo, n2g, rwh, rwl, rb, tri)


def _wprep_kernel(w1_ref, w2_ref, perm_ref, g_ref, l_ref, w2o_ref):
    for j in range(w1_ref.shape[-1] // MXU_DIM):
        z = jnp.dot(w1_ref[0, 0, :, j * MXU_DIM:(j + 1) * MXU_DIM].astype(BF16), perm_ref[...],
                    preferred_element_type=F32)
        g_ref[0, :, j * LANES:(j + 1) * LANES] = z[:, :LANES].astype(BF16)
        l_ref[0, :, j * LANES:(j + 1) * LANES] = z[:, LANES:].astype(BF16)
    w2o_ref[0] = w2_ref[0, 0].astype(BF16)


def _wprep(layer, w1, w2):
    _, n_e, d, f2 = w1.shape
    f = f2 // 2
    src = jnp.concatenate([jnp.arange(0, MXU_DIM, 2), jnp.arange(1, MXU_DIM, 2)])
    perm = (jnp.arange(MXU_DIM)[:, None] == src[None, :]).astype(BF16)
    return pl.pallas_call(
        _wprep_kernel,
        grid=(n_e,),
        in_specs=[pl.BlockSpec((1, 1, d, f2), lambda e: (layer, e, 0, 0)),
                  pl.BlockSpec((1, 1, f, d), lambda e: (layer, e, 0, 0)),
                  _const_spec((MXU_DIM, MXU_DIM))],
        out_specs=[pl.BlockSpec((1, d, f), lambda e: (e, 0, 0)),
                   pl.BlockSpec((1, d, f), lambda e: (e, 0, 0)),
                   pl.BlockSpec((1, f, d), lambda e: (e, 0, 0))],
        out_shape=[jax.ShapeDtypeStruct((n_e, d, f), BF16),
                   jax.ShapeDtypeStruct((n_e, d, f), BF16),
                   jax.ShapeDtypeStruct((n_e, f, d), BF16)],
        compiler_params=pltpu.CompilerParams(dimension_semantics=("parallel",),
                                             vmem_limit_bytes=VMEM_LIMIT),
        name="wprep",
    )(w1, w2, perm)


def _expert_kernel(be_ref, nu_ref, x_ref, w1g_ref, w1l_ref, b1g_ref, b1l_ref, w2_ref, b2_ref, y_ref):
    del be_ref

    @pl.when(pl.program_id(0) < nu_ref[0])
    def _():
        x = x_ref[...].astype(BF16)
        glu = jnp.minimum(jnp.dot(x, w1g_ref[0], preferred_element_type=F32) + b1g_ref[0], SWIGLU_LIMIT)
        lin = jnp.clip(jnp.dot(x, w1l_ref[0], preferred_element_type=F32) + b1l_ref[0],
                       -SWIGLU_LIMIT, SWIGLU_LIMIT)
        act = glu * jax.nn.sigmoid(SWIGLU_ALPHA * glu) * (lin + 1.0)
        y_ref[...] = jnp.dot(act.astype(BF16), w2_ref[0], preferred_element_type=F32) + b2_ref[0]

    @pl.when(pl.program_id(0) >= nu_ref[0])
    def _():
        y_ref[...] = jnp.zeros(y_ref.shape, F32)


def _experts(block_expert, n_used, xs, w1g, w1l, b1g, b1l, w2, b2, *, tb):
    n_slots, d = xs.shape
    n_e, _, f = w1g.shape
    emap = lambda i, be, nu: (be[i], 0, 0)
    grid_spec = pltpu.PrefetchScalarGridSpec(
        num_scalar_prefetch=2,
        grid=(n_slots // tb,),
        in_specs=[pl.BlockSpec((tb, d), lambda i, be, nu: (jnp.minimum(i, nu[0] - 1), 0)),
                  pl.BlockSpec((1, d, f), emap),
                  pl.BlockSpec((1, d, f), emap),
                  pl.BlockSpec((1, 1, f), emap),
                  pl.BlockSpec((1, 1, f), emap),
                  pl.BlockSpec((1, f, d), emap),
                  pl.BlockSpec((1, 1, d), emap)],
        out_specs=pl.BlockSpec((tb, d), lambda i, be, nu: (i, 0)))
    return pl.pallas_call(
        _expert_kernel,
        grid_spec=grid_spec,
        out_shape=jax.ShapeDtypeStruct((n_slots, d), F32),
        compiler_params=pltpu.CompilerParams(dimension_semantics=("arbitrary",),
                                             vmem_limit_bytes=VMEM_LIMIT),
        name="experts",
    )(block_expert, n_used, xs, w1g, w1l, b1g.reshape(n_e, 1, f), b1l.reshape(n_e, 1, f), w2,
      b2.reshape(n_e, 1, d))


def _dispatch_kernel(dest_ref, h_ref, xs_init_ref, xs_ref, sem):
    del xs_init_ref
    tm = h_ref.shape[0]

    def issue(t, carry):
        for k in range(TOP_K):
            slot = dest_ref[0, 0, t * TOP_K + k]
            pltpu.make_async_copy(h_ref.at[pl.ds(t, 1)], xs_ref.at[pl.ds(slot, 1)], sem).start()
        return carry

    lax.fori_loop(0, tm, issue, 0)
    for _ in range(TOP_K):
        pltpu.make_async_copy(h_ref, xs_ref.at[pl.ds(0, tm)], sem).wait()


def _dispatch(dest_tiles, h, n_slots):
    t, d = h.shape
    n_tiles, _, n = dest_tiles.shape
    tm = n // TOP_K
    return pl.pallas_call(
        _dispatch_kernel,
        grid=(n_tiles,),
        in_specs=[pl.BlockSpec((1, 1, n), lambda i: (i, 0, 0), memory_space=pltpu.SMEM),
                  pl.BlockSpec((tm, d), lambda i: (i, 0)),
                  pl.BlockSpec(memory_space=pl.ANY)],
        out_specs=pl.BlockSpec(memory_space=pl.ANY),
        out_shape=jax.ShapeDtypeStruct((n_slots, d), F32),
        scratch_shapes=[pltpu.SemaphoreType.DMA(())],
        input_output_aliases={2: 0},
        compiler_params=pltpu.CompilerParams(dimension_semantics=("arbitrary",)),
        name="dispatch",
    )(dest_tiles, h, jnp.zeros((n_slots, d), F32))


def _combine_kernel(dcur_ref, dnext_ref, gate_ref, x_ref, g2_ref, yb_ref, o_ref, buf_ref, sem):
    i = pl.program_id(0)
    tm = x_ref.shape[0]

    def gather(d_ref, slot):
        def issue(t, carry):
            for k in range(TOP_K):
                pltpu.make_async_copy(yb_ref.at[pl.ds(d_ref[0, 0, t * TOP_K + k], 1)],
                                      buf_ref.at[slot, k, pl.ds(t, 1)], sem.at[slot]).start()
            return carry
        lax.fori_loop(0, tm, issue, 0)

    @pl.when(i == 0)
    def _():
        gather(dcur_ref, 0)

    slot = i % 2

    @pl.when(i + 1 < pl.num_programs(0))
    def _():
        gather(dnext_ref, 1 - slot)

    for k in range(TOP_K):
        pltpu.make_async_copy(yb_ref.at[pl.ds(0, tm)], buf_ref.at[slot, k], sem.at[slot]).wait()
    gates = gate_ref[...]
    y = buf_ref[slot, 0] * gates[:, 0:1]
    for k in range(1, TOP_K):
        y = y + buf_ref[slot, k] * gates[:, k:k + 1]
    o_ref[...] = x_ref[...] + g2_ref[0] * y


def _combine(dest_tiles, gates, x, g2, yb, *, tiles_per_seq):
    t, d = x.shape
    n_tiles, _, n = dest_tiles.shape
    tm = n // TOP_K
    if g2.shape[1] == 1:
        g2_spec = pl.BlockSpec((1, 1, d), lambda i: (i // tiles_per_seq, 0, 0))
    else:
        g2_spec = pl.BlockSpec((1, tm, d), lambda i: (i // tiles_per_seq, i % tiles_per_seq, 0))
    dspec = lambda f: pl.BlockSpec((1, 1, n), f, memory_space=pltpu.SMEM)
    return pl.pallas_call(
        _combine_kernel,
        grid=(n_tiles,),
        in_specs=[dspec(lambda i: (i, 0, 0)),
                  dspec(lambda i: (jnp.minimum(i + 1, n_tiles - 1), 0, 0)),
                  pl.BlockSpec((tm, LANES), lambda i: (i, 0)),
                  pl.BlockSpec((tm, d), lambda i: (i, 0)),
                  g2_spec,
                  pl.BlockSpec(memory_space=pl.ANY)],
        out_specs=pl.BlockSpec((tm, d), lambda i: (i, 0)),
        out_shape=jax.ShapeDtypeStruct((t, d), F32),
        scratch_shapes=[pltpu.VMEM((2, TOP_K, tm, d), F32),
                        pltpu.SemaphoreType.DMA((2,))],
        compiler_params=pltpu.CompilerParams(dimension_semantics=("arbitrary",),
                                             vmem_limit_bytes=VMEM_LIMIT),
        name="combine",
    )(dest_tiles, dest_tiles, gates, x, g2, yb)


def _moe(h2, idx, gates, rank, cnt, x, g2, w1g, w1l, b1g, b1l, w2, b2, *, tb, tm):
    nb, s, d = h2.shape
    t = nb * s
    n_assign = t * TOP_K
    counts = cnt[0, :N_EXPERTS].astype(jnp.int32)
    padded = (counts + tb - 1) // tb * tb
    pad_ends = jnp.cumsum(padded)
    pad_starts = pad_ends - padded
    idx4 = idx.reshape(t, LANES)[:, :TOP_K]
    rank4 = rank.reshape(t, LANES)[:, :TOP_K]
    experts = jnp.arange(N_EXPERTS, dtype=jnp.int32)
    dest = rank4 + jnp.sum(jnp.where(idx4[:, :, None] == experts, pad_starts, 0), axis=-1)
    dest_tiles = dest.reshape(t // tm, 1, tm * TOP_K)
    n_blocks = -(-n_assign // tb) + N_EXPERTS
    block_start = jnp.arange(n_blocks, dtype=jnp.int32) * tb
    block_expert = jnp.minimum(jnp.sum((block_start[:, None] >= pad_ends[None, :]).astype(jnp.int32), axis=1),
                               N_EXPERTS - 1)
    n_used = (pad_ends[-1:] // tb).astype(jnp.int32)
    xs = _dispatch(dest_tiles, h2.reshape(t, d), n_blocks * tb)
    yb = _experts(block_expert, n_used, xs, w1g, w1l, b1g, b1l, w2, b2, tb=tb)
    out = _combine(dest_tiles, gates.reshape(t, LANES), x.reshape(t, d), g2, yb, tiles_per_seq=s // tm)
    return out.reshape(nb, s, d)


def _lambda_init(layer):
    return 0.8 - 0.6 * math.exp(-0.3 * layer)


def _decode_bias(n_pos, visible):
    col = jnp.arange(n_pos * KV_HEADS, dtype=jnp.int32)[None, :]
    row = jnp.arange(2 * N_HEADS, dtype=jnp.int32)[:, None]
    ok = ((col // n_pos) == ((row % N_HEADS) // GQA_GROUP)) & ((col % n_pos) < visible)
    return jnp.where(ok, 0.0, NEG).astype(F32)


def kernel(x_prompt, x_sample, c_prompt, c_sample, cache_k, cache_v, page_table, ada_w, ada_b, norm1_g, norm2_g, w_in, q_norm_g, k_norm_g, lambda_q1, lambda_k1, lambda_q2, lambda_k2, subln_g, ln_v_g, ln_v_b, w_spatial, b_spatial, w_branch_a, w_branch_b, w_out, router_w, router_b, w1, b1, w2, b2):
    depth = ada_w.shape[0]
    nb, seq, d = x_prompt.shape
    db, ds, _ = x_sample.shape
    assert ds == 1, "sampled tokens open a fresh chunk one row at a time"
    n_phys, page = cache_k.shape[1], cache_k.shape[2]

    mods = _ada_mods(jnp.concatenate([c_prompt, c_sample], axis=0), ada_w, ada_b)
    ck = jnp.transpose(cache_k, (0, 1, 3, 4, 5, 2))
    cv = cache_v.reshape(depth, n_phys, page * KV_HEADS, V_DIM)
    bias = _decode_bias(page, page)
    bias_new = _decode_bias(page, ds)
    tri = jnp.tril(jnp.ones((CHUNK, CHUNK), F32))

    yp = x_prompt
    ys = x_sample.reshape(1, db, d)
    kp_l, vp_l, ks_l, vs_l, cv_l = [], [], [], [], []
    for l in range(depth):
        lam_init = _lambda_init(l)
        lam = (jnp.exp(jnp.sum(lambda_q1[l] * lambda_k1[l])) - jnp.exp(jnp.sum(lambda_q2[l] * lambda_k2[l]))
               + lam_init).reshape(1).astype(F32)
        m6 = mods[l].reshape(nb + db, 6, d)
        mp = [m6[:nb, i][:, None, :] for i in range(6)]
        msm = [m6[nb:, i][None, :, :] for i in range(6)]
        w_in_b = w_in[l].astype(BF16)
        n1g = norm1_g[l][None, :]
        n2g = norm2_g[l][None, :]
        lng, lnb = ln_v_g[l][None, :], ln_v_b[l][None, :]
        qg = jnp.tile(q_norm_g[l], 2)[None, :]
        kg = jnp.tile(k_norm_g[l], 2)[None, :]
        sg = subln_g[l][None, :]
        sg_col = subln_g[l][:, None]
        ws_p = (w_spatial[l] * tri).astype(BF16)
        bs_p = jnp.repeat(b_spatial[l].T, LANES, axis=1)
        ws_s = jnp.repeat(w_spatial[l][:, 0, 0], LANES)[None, :]
        bs_s = jnp.repeat(b_spatial[l][:, 0], LANES)[None, :]
        wa, wb, wo = (w_branch_a[l].astype(BF16), w_branch_b[l].astype(BF16), w_out[l].astype(BF16))
        rw = jnp.pad(router_w[l], ((0, 0), (0, LANES - N_EXPERTS)))
        rwh, rwl = _split_bf16(rw)
        rb = jnp.pad(router_b[l], (0, LANES - N_EXPERTS), constant_values=NEG)[None, :]
        w1g, w1l, w2p = _wprep(l, w1, w2)
        b1g, b1l = b1[l][:, 0::2], b1[l][:, 1::2]
        ew = (w1g, w1l, b1g, b1l, w2p, b2[l])
        out_scale = 1.0 - lam_init

        mixed, q, k, v, kb, vb, ga, gb = _inproj(yp, mp[1], mp[0], n1g, w_in_b, lng, lnb, qg, kg,
                                                 ws_p, bs_p, sample=False, tm=TM_DENSE,
                                                 q_scale=HEAD_DIM ** -0.5 * LOG2E)
        o = _attn_prompt(lam, q, kb, vb, sg_col, out_scale=out_scale, tq=TQ, tk=TK)
        xo, h2, *route = _merge(mixed, o, ga, gb, yp, mp[2], mp[4], mp[3], wa, wb, wo, n2g, rwh, rwl, rb,
                                tm=TM_DENSE)
        yp = _moe(h2, *route, xo, mp[5], *ew, tb=TB_PROMPT, tm=TM_DENSE)
        kp_l.append(k.reshape(nb, seq, KV_HEADS, 2, HEAD_DIM))
        vp_l.append(v.reshape(nb, seq, KV_HEADS, V_DIM))

        mixed, q, k, v, kb, vb, ga, gb, cvs = _inproj(ys, msm[1], msm[0], n1g, w_in_b, lng, lnb, qg, kg,
                                                      ws_s, bs_s, sample=True, tm=db,
                                                      q_scale=HEAD_DIM ** -0.5)
        q16 = q.reshape(db, N_HEADS, 2, HEAD_DIM).transpose(0, 2, 1, 3).reshape(db, 2 * N_HEADS, HEAD_DIM)
        k_new = jnp.pad(k.reshape(db, KV_HEADS, 2, HEAD_DIM, 1), ((0, 0),) * 4 + ((0, page - 1),))
        v_new = jnp.pad(v.reshape(db, KV_HEADS, V_DIM), ((0, 0), (0, (page - 1) * KV_HEADS), (0, 0)))
        o = _attn_decode(l, page_table, lam, q16, bias, bias_new, k_new, v_new, sg, ck, cv,
                         out_scale=out_scale)
        o = o.reshape(1, db, d).astype(BF16)
        xo, h2, *route = _merge(mixed, o, ga, gb, ys, msm[2], msm[4], msm[3], wa, wb, wo, n2g, rwh, rwl, rb,
                                tm=db)
        ys = _moe(h2, *route, xo, msm[5], *ew, tb=TB_SAMPLE, tm=db)
        ks_l.append(k.reshape(db, ds, KV_HEADS, 2, HEAD_DIM))
        vs_l.append(v.reshape(db, ds, KV_HEADS, V_DIM))
        cv_l.append(cvs.reshape(db, ds, d))

    return (yp, ys.reshape(db, ds, d), jnp.stack(kp_l), jnp.stack(vp_l), jnp.stack(ks_l),
            jnp.stack(vs_l), jnp.stack(cv_l))
```

```python
import functools
import math

import jax
import jax.numpy as jnp
from jax import lax
from jax.experimental import pallas as pl
from jax.experimental.pallas import tpu as pltpu

F32 = jnp.float32
BF16 = jnp.bfloat16

LANES = 128
MXU_DIM = 256
HEAD_DIM = 64
N_HEADS = 8
KV_HEADS = 4
GQA_GROUP = N_HEADS // KV_HEADS
V_DIM = 2 * HEAD_DIM
CHUNK = 128
A_GROUPS = 8
N_EXPERTS = 32
TOP_K = 4
SUB = 8
SWIGLU_ALPHA = 1.702
SWIGLU_LIMIT = 7.0
LOG2E = math.log2(math.e)
NEG = -1e30

TM_DENSE = 256
TQ = 512
TK = 512
ONES_ROWS = 16
PAGES_PER_STEP = 16
TB_PROMPT = 256
TB_SAMPLE = 16
VMEM_LIMIT = 48 * 1024 * 1024


def _const_spec(shape):
    nd = len(shape)
    return pl.BlockSpec(shape, lambda *_: (0,) * nd, pipeline_mode=pl.Buffered(1))


def _gelu(x):
    return x * (lax.erf(x * (1.0 / math.sqrt(2.0))) + 1.0) * 0.5


def _rms(x, eps=1e-6):
    return x * lax.rsqrt(jnp.mean(x * x, axis=-1, keepdims=True) + eps)


def _ada_kernel(c_ref, w_ref, b_ref, o_ref):
    c = c_ref[...]
    sc = (c * jax.nn.sigmoid(c)).astype(BF16)
    o_ref[0] = jnp.dot(sc, w_ref[0].astype(BF16), preferred_element_type=F32) + b_ref[0]


def _ada_mods(c_all, ada_w, ada_b):
    depth, d, n = ada_w.shape
    r = c_all.shape[0]
    tn = 512
    return pl.pallas_call(
        _ada_kernel,
        grid=(depth, n // tn),
        in_specs=[pl.BlockSpec((r, d), lambda l, j: (0, 0)),
                  pl.BlockSpec((1, d, tn), lambda l, j: (l, 0, j)),
                  pl.BlockSpec((1, 1, tn), lambda l, j: (l, 0, j))],
        out_specs=pl.BlockSpec((1, r, tn), lambda l, j: (l, 0, j)),
        out_shape=jax.ShapeDtypeStruct((depth, r, n), F32),
        compiler_params=pltpu.CompilerParams(dimension_semantics=("parallel", "parallel")),
        name="ada",
    )(c_all, ada_w, ada_b.reshape(depth, 1, n))


def _half_rms(t, g_row, first):
    s = t * t
    s1 = jnp.sum(jnp.where(first, s, 0.0), axis=-1, keepdims=True)
    s2 = jnp.sum(jnp.where(first, 0.0, s), axis=-1, keepdims=True)
    r = jnp.where(first, lax.rsqrt(s1 * (1.0 / HEAD_DIM) + 1e-6),
                  lax.rsqrt(s2 * (1.0 / HEAD_DIM) + 1e-6))
    return t * r * g_row


def _inproj_kernel(x_ref, sc_ref, sh_ref, n1g_ref, w_ref, lng_ref, lnb_ref, qg_ref, kg_ref,
                   ws_ref, bs_ref, mixed_ref, q_ref, k_ref, v_ref, kb_ref, vb_ref, ga_ref, gb_ref,
                   *cv_ref, sample, q_scale):
    d = x_ref.shape[-1]
    tm = x_ref.shape[1]
    x = x_ref[0]
    h = (_rms(x) * n1g_ref[...] * (1.0 + sc_ref[0]) + sh_ref[0]).astype(BF16)

    def proj(lo, width):
        return jnp.dot(h, w_ref[:, lo:lo + width], preferred_element_type=F32)

    u = _gelu(proj(0, d))
    va = _gelu(proj(d, d))
    dv = va - jnp.mean(va, axis=-1, keepdims=True)
    van = dv * lax.rsqrt(jnp.mean(dv * dv, axis=-1, keepdims=True) + 1e-5) * lng_ref[...] + lnb_ref[...]
    if sample:
        cv_ref[0][0] = van
        mixed_ref[0] = (u * (van * ws_ref[...] + bs_ref[...])).astype(BF16)
    else:
        vb16 = van.astype(BF16)
        for c in range(tm // CHUNK):
            rs = slice(c * CHUNK, (c + 1) * CHUNK)
            for g in range(A_GROUPS):
                cs = slice(g * LANES, (g + 1) * LANES)
                mix = jnp.dot(ws_ref[g], vb16[rs, cs], preferred_element_type=F32) + bs_ref[:, cs]
                mixed_ref[0, rs, cs] = (u[rs, cs] * mix).astype(BF16)

    first = lax.broadcasted_iota(jnp.int32, (1, LANES), 1) < HEAD_DIM
    zq = proj(2 * d, N_HEADS * LANES)
    for hh in range(N_HEADS):
        cs = slice(hh * LANES, (hh + 1) * LANES)
        q_ref[0, :, cs] = (_half_rms(zq[:, cs], qg_ref[...], first) * q_scale).astype(BF16)
    zk = proj(2 * d + N_HEADS * LANES, KV_HEADS * LANES)
    for hh in range(KV_HEADS):
        cs = slice(hh * LANES, (hh + 1) * LANES)
        kn = _half_rms(zk[:, cs], kg_ref[...], first)
        k_ref[0, :, cs] = kn
        kb_ref[0, :, cs] = kn.astype(BF16)
    zv = proj(2 * d + (N_HEADS + KV_HEADS) * LANES, KV_HEADS * LANES)
    v_ref[0] = zv
    vb_ref[0] = zv.astype(BF16)
    base = 2 * d + (N_HEADS + 2 * KV_HEADS) * LANES
    ga_ref[0] = jax.nn.sigmoid(proj(base, d)).astype(BF16)
    gb_ref[0] = jax.nn.sigmoid(proj(base + d, d)).astype(BF16)


def _inproj(x, sc, sh, n1g, w_in, lng, lnb, qg, kg, ws, bs, *, sample, tm, q_scale):
    nb, s, d = x.shape
    r = sc.shape[1]
    kw = KV_HEADS * LANES
    row = lambda b, i: (b, i, 0)
    mod = (lambda b, i: (b, i, 0)) if r == s else (lambda b, i: (b, 0, 0))
    rblk = tm if r == s else 1
    in_specs = [pl.BlockSpec((1, tm, d), row),
                pl.BlockSpec((1, rblk, d), mod),
                pl.BlockSpec((1, rblk, d), mod),
                _const_spec((1, d)),
                _const_spec(w_in.shape),
                _const_spec((1, d)), _const_spec((1, d)),
                _const_spec((1, LANES)), _const_spec((1, LANES)),
                _const_spec(ws.shape), _const_spec(bs.shape)]
    out_shapes = [jax.ShapeDtypeStruct((nb, s, d), BF16),
                  jax.ShapeDtypeStruct((nb, s, d), BF16),
                  jax.ShapeDtypeStruct((nb, s, kw), F32),
                  jax.ShapeDtypeStruct((nb, s, kw), F32),
                  jax.ShapeDtypeStruct((nb, s, kw), BF16),
                  jax.ShapeDtypeStruct((nb, s, kw), BF16),
                  jax.ShapeDtypeStruct((nb, s, d), BF16),
                  jax.ShapeDtypeStruct((nb, s, d), BF16)]
    if sample:
        out_shapes.append(jax.ShapeDtypeStruct((nb, s, d), F32))
    out_specs = [pl.BlockSpec((1, tm, o.shape[-1]), row) for o in out_shapes]
    return pl.pallas_call(
        functools.partial(_inproj_kernel, sample=sample, q_scale=q_scale),
        grid=(nb, s // tm),
        in_specs=in_specs,
        out_specs=out_specs,
        out_shape=out_shapes,
        compiler_params=pltpu.CompilerParams(dimension_semantics=("parallel", "parallel"),
                                             vmem_limit_bytes=VMEM_LIMIT),
        name="inproj_sample" if sample else "inproj",
    )(x, sc, sh, n1g, w_in, lng, lnb, qg, kg, ws, bs)


def _attn_kernel(lam_ref, q_ref, k_ref, v_ref, sg_ref, o_ref, qs_ref, vt_ref, sa_ref, sb_ref, m_ref, acc_ref,
                 *, tq, tk, out_scale):
    qi = pl.program_id(2)
    n_kt = vt_ref.shape[0]

    @pl.when(qi == 0)
    def _():
        def fill(j, carry):
            start = pl.multiple_of(j * tk, tk)
            vt_ref[j, :LANES, :] = v_ref[0, pl.ds(start, tk), :].astype(F32).T.astype(BF16)
            vt_ref[j, LANES:, :] = jnp.ones((ONES_ROWS, tk), BF16)
            return carry
        lax.fori_loop(0, n_kt, fill, 0)

    first = lax.broadcasted_iota(jnp.int32, (LANES, 1), 0) < HEAD_DIM
    for g in range(GQA_GROUP):
        qt = q_ref[0, :, g * LANES:(g + 1) * LANES].astype(F32).T
        qs_ref[:, (2 * g) * tq:(2 * g + 1) * tq] = jnp.where(first, qt, 0.0).astype(BF16)
        qs_ref[:, (2 * g + 1) * tq:(2 * g + 2) * tq] = jnp.where(first, 0.0, qt).astype(BF16)
    m_ref[...] = jnp.full(m_ref.shape, NEG, F32)
    acc_ref[...] = jnp.zeros(acc_ref.shape, F32)

    def qk_into(s_ref, kj):
        start = pl.multiple_of(kj * tk, tk)
        s_ref[...] = jnp.dot(k_ref[0, pl.ds(start, tk), :], qs_ref[...], preferred_element_type=F32)

    def process(s_ref, kj, masked):
        s = s_ref[...]
        if masked:
            kpos = kj * tk + lax.broadcasted_iota(jnp.int32, s.shape, 0)
            qpos = qi * tq + (lax.broadcasted_iota(jnp.int32, s.shape, 1) & (tq - 1))
            s = jnp.where(kpos <= qpos, s, NEG)
        m_prev = m_ref[...]
        m_new = jnp.maximum(m_prev, jnp.max(s, axis=0, keepdims=True))
        alpha = jnp.exp2(m_prev - m_new)
        p = jnp.exp2(s - m_new)
        acc_ref[...] = alpha * acc_ref[...] + jnp.dot(vt_ref[kj], p.astype(BF16),
                                                      preferred_element_type=F32)
        m_ref[...] = m_new

    n_full = (qi * tq) // tk
    qk_into(sa_ref, 0)

    def pair(i, carry):
        qk_into(sb_ref, 2 * i + 1)
        process(sa_ref, 2 * i, False)
        qk_into(sa_ref, 2 * i + 2)
        process(sb_ref, 2 * i + 1, False)
        return carry

    lax.fori_loop(0, n_full // 2, pair, 0)

    @pl.when(n_full % 2 == 1)
    def _():
        qk_into(sb_ref, n_full)
        process(sa_ref, n_full - 1, False)
        process(sb_ref, n_full, True)

    @pl.when(n_full % 2 == 0)
    def _():
        process(sa_ref, n_full, True)

    o_all = acc_ref[:LANES, :] / acc_ref[LANES:LANES + 1, :]
    lam = lam_ref[0]
    for g in range(GQA_GROUP):
        og = o_all[:, (2 * g) * tq:(2 * g + 1) * tq] - lam * o_all[:, (2 * g + 1) * tq:(2 * g + 2) * tq]
        og = og * lax.rsqrt(jnp.mean(og * og, axis=0, keepdims=True) + 1e-6) * sg_ref[...] * out_scale
        o_ref[0, :, g * LANES:(g + 1) * LANES] = og.T.astype(BF16)


def _attn_prompt(lam, q, kb, vb, sg_col, *, out_scale, tq, tk):
    nb, s, d = q.shape
    gw = GQA_GROUP * LANES
    m = 2 * GQA_GROUP * tq
    return pl.pallas_call(
        functools.partial(_attn_kernel, tq=tq, tk=tk, out_scale=out_scale),
        grid=(nb, KV_HEADS, s // tq),
        in_specs=[pl.BlockSpec(memory_space=pltpu.SMEM),
                  pl.BlockSpec((1, tq, gw), lambda b, h, i: (b, i, h)),
                  pl.BlockSpec((1, s, LANES), lambda b, h, i: (b, 0, h)),
                  pl.BlockSpec((1, s, LANES), lambda b, h, i: (b, 0, h)),
                  _const_spec((LANES, 1))],
        out_specs=pl.BlockSpec((1, tq, gw), lambda b, h, i: (b, i, h)),
        out_shape=jax.ShapeDtypeStruct((nb, s, d), BF16),
        scratch_shapes=[pltpu.VMEM((LANES, m), BF16),
                        pltpu.VMEM((s // tk, LANES + ONES_ROWS, tk), BF16),
                        pltpu.VMEM((tk, m), F32),
                        pltpu.VMEM((tk, m), F32),
                        pltpu.VMEM((1, m), F32),
                        pltpu.VMEM((LANES + ONES_ROWS, m), F32)],
        compiler_params=pltpu.CompilerParams(
            dimension_semantics=("parallel", "parallel", "arbitrary"),
            vmem_limit_bytes=VMEM_LIMIT),
        name="attn_prompt",
    )(lam, q, kb, vb, sg_col)


def _decode_kernel(pt_ref, lam_ref, q_ref, bias_ref, biasn_ref, kn_ref, vn_ref, sg_ref, *rest,
                   n_pg, out_scale):
    k_refs = rest[:n_pg]
    v_refs = rest[n_pg:2 * n_pg]
    o_ref = rest[2 * n_pg]
    m_ref, l_ref, acc_ref = rest[2 * n_pg + 1:]
    del pt_ref
    j = pl.program_id(1)
    half = N_HEADS

    @pl.when(j == 0)
    def _():
        m_ref[...] = jnp.full(m_ref.shape, NEG, F32)
        l_ref[...] = jnp.zeros(l_ref.shape, F32)
        acc_ref[...] = jnp.zeros(acc_ref.shape, F32)

    q16 = q_ref[0]

    def scores(kref, idx, bias):
        parts = []
        for mp in range(2):
            kcat = jnp.concatenate([kref[idx + (kvh, mp)] for kvh in range(KV_HEADS)], axis=1)
            sm = jnp.dot(q16, kcat.astype(BF16), preferred_element_type=F32)
            parts.append(sm[mp * half:(mp + 1) * half])
        return jnp.concatenate(parts, axis=0) + bias

    def values(vref, idx):
        n = vref.shape[-2] // KV_HEADS
        return jnp.concatenate([vref[idx + (pl.ds(kvh, n, stride=KV_HEADS), slice(None))]
                                for kvh in range(KV_HEADS)], axis=0).astype(BF16)

    def update(s_list, v_list):
        s = jnp.concatenate(s_list, axis=1) if len(s_list) > 1 else s_list[0]
        m_prev = m_ref[...]
        m_new = jnp.maximum(m_prev, jnp.max(s, axis=-1, keepdims=True))
        alpha = jnp.exp(m_prev - m_new)
        p = jnp.exp(s - m_new)
        l_ref[...] = alpha * l_ref[...] + jnp.sum(p, axis=-1, keepdims=True)
        w = s_list[0].shape[1]
        pv = None
        for i, vv in enumerate(v_list):
            t = jnp.dot(p[:, i * w:(i + 1) * w].astype(BF16), vv, preferred_element_type=F32)
            pv = t if pv is None else pv + t
        acc_ref[...] = alpha * acc_ref[...] + pv
        m_ref[...] = m_new

    bias = bias_ref[...]
    update([scores(kr, (0, 0), bias) for kr in k_refs], [values(vr, (0, 0)) for vr in v_refs])

    @pl.when(j == pl.num_programs(1) - 1)
    def _():
        update([scores(kn_ref, (0,), biasn_ref[...])], [values(vn_ref, (0,))])
        o16 = acc_ref[...] / l_ref[...]
        og = o16[:half] - lam_ref[0] * o16[half:]
        o_ref[0] = _rms(og) * sg_ref[...] * out_scale


def _attn_decode(layer, page_table, lam, q16, bias, bias_new, k_new, v_new, sg, ck, cv, *, out_scale):
    db, n_pages = page_table.shape
    n_pg = PAGES_PER_STEP
    page = ck.shape[-1]
    rows_v = cv.shape[2]
    pt_flat = page_table.reshape(-1)

    def kmap(i):
        return lambda b, j, pt: (layer, pt[b * n_pages + j * n_pg + i], 0, 0, 0, 0)

    def vmap(i):
        return lambda b, j, pt: (layer, pt[b * n_pages + j * n_pg + i], 0, 0)

    per_b = lambda b, j, pt: (b, 0, 0)
    const2 = lambda b, j, pt: (0, 0)
    kblk = (KV_HEADS, 2, HEAD_DIM, page)
    in_specs = [pl.BlockSpec(memory_space=pltpu.SMEM),
                pl.BlockSpec((1, 2 * N_HEADS, HEAD_DIM), per_b),
                pl.BlockSpec(bias.shape, const2),
                pl.BlockSpec(bias_new.shape, const2),
                pl.BlockSpec((1,) + kblk, lambda b, j, pt: (b, 0, 0, 0, 0)),
                pl.BlockSpec((1, rows_v, V_DIM), per_b),
                pl.BlockSpec((1, LANES), const2)]
    in_specs += [pl.BlockSpec((1, 1) + kblk, kmap(i)) for i in range(n_pg)]
    in_specs += [pl.BlockSpec((1, 1, rows_v, V_DIM), vmap(i)) for i in range(n_pg)]
    grid_spec = pltpu.PrefetchScalarGridSpec(
        num_scalar_prefetch=1,
        grid=(db, n_pages // n_pg),
        in_specs=in_specs,
        out_specs=pl.BlockSpec((1, N_HEADS, V_DIM), per_b),
        scratch_shapes=[pltpu.VMEM((2 * N_HEADS, 1), F32),
                        pltpu.VMEM((2 * N_HEADS, 1), F32),
                        pltpu.VMEM((2 * N_HEADS, V_DIM), F32)])
    return pl.pallas_call(
        functools.partial(_decode_kernel, n_pg=n_pg, out_scale=out_scale),
        grid_spec=grid_spec,
        out_shape=jax.ShapeDtypeStruct((db, N_HEADS, V_DIM), F32),
        compiler_params=pltpu.CompilerParams(dimension_semantics=("parallel", "arbitrary"),
                                             vmem_limit_bytes=VMEM_LIMIT),
        name="attn_decode",
    )(pt_flat, lam, q16, bias, bias_new, k_new, v_new, sg, *([ck] * n_pg), *([cv] * n_pg))


def _split_bf16(x):
    hi = x.astype(BF16)
    return hi, (x - hi.astype(F32)).astype(BF16)


def _merge_kernel(mixed_ref, o_ref, ga_ref, gb_ref, x_ref, g1_ref, sc_ref, sh_ref, wa_ref, wb_ref,
                  wo_ref, n2g_ref, rwh_ref, rwl_ref, rb_ref, tri_ref,
                  xo_ref, h2_ref, idx_ref, gate_ref, rank_ref, cnt_ref, run_ref):
    @pl.when((pl.program_id(0) == 0) & (pl.program_id(1) == 0))
    def _():
        run_ref[...] = jnp.zeros(run_ref.shape, F32)

    a = jnp.dot(mixed_ref[0], wa_ref[...], preferred_element_type=F32)
    b = jnp.dot(o_ref[0], wb_ref[...], preferred_element_type=F32)
    merged = ga_ref[0].astype(F32) * a + gb_ref[0].astype(F32) * b
    xo = x_ref[0] + g1_ref[0] * jnp.dot(merged.astype(BF16), wo_ref[...], preferred_element_type=F32)
    xo_ref[0] = xo
    h2 = _rms(xo) * n2g_ref[...] * (1.0 + sc_ref[0]) + sh_ref[0]
    hi, lo = _split_bf16(h2)
    tm = hi.shape[0]
    hi32 = hi.astype(F32)
    for j in range(SUB):
        h2_ref[0, pl.ds(j, tm, stride=SUB), :] = hi32[:, j * LANES:(j + 1) * LANES]
    lg = (jnp.dot(hi, rwh_ref[...], preferred_element_type=F32)
          + jnp.dot(lo, rwh_ref[...], preferred_element_type=F32)
          + jnp.dot(hi, rwl_ref[...], preferred_element_type=F32)
          + rb_ref[...])

    lane = lax.broadcasted_iota(jnp.int32, lg.shape, 1)
    work = lg
    sels, vals, idxs = [], [], []
    for _ in range(TOP_K):
        mx = jnp.max(work, axis=-1, keepdims=True)
        ik = jnp.min(jnp.where(work == mx, lane, LANES), axis=-1, keepdims=True)
        sel = lane == ik
        work = jnp.where(sel, -jnp.inf, work)
        sels.append(sel)
        vals.append(mx)
        idxs.append(ik)
    es = [jnp.exp(v - vals[0]) for v in vals]
    den = es[0]
    for e in es[1:]:
        den = den + e

    onehot = sels[0].astype(F32)
    for sel in sels[1:]:
        onehot = onehot + sel.astype(F32)
    base = jnp.dot(tri_ref[...], onehot.astype(BF16), preferred_element_type=F32) + run_ref[...]
    idx_out = jnp.zeros(lg.shape, jnp.int32)
    gate_out = jnp.zeros(lg.shape, F32)
    rank_out = jnp.zeros(lg.shape, jnp.int32)
    for k in range(TOP_K):
        rk = jnp.sum(jnp.where(sels[k], base, 0.0), axis=-1, keepdims=True)
        idx_out = jnp.where(lane == k, idxs[k], idx_out)
        gate_out = jnp.where(lane == k, es[k] / den, gate_out)
        rank_out = jnp.where(lane == k, rk.astype(jnp.int32), rank_out)
    idx_ref[0] = idx_out
    gate_ref[0] = gate_out
    rank_ref[0] = rank_out
    run_ref[...] = run_ref[...] + jnp.sum(onehot, axis=0, keepdims=True)
    cnt_ref[...] = run_ref[...]


def _merge(mixed, o, ga, gb, x, g1, sc, sh, wa, wb, wo, n2g, rwh, rwl, rb, *, tm):
    nb, s, d = x.shape
    r = g1.shape[1]
    row = lambda b, i: (b, i, 0)
    mod = (lambda b, i: (b, i, 0)) if r == s else (lambda b, i: (b, 0, 0))
    rblk = tm if r == s else 1
    tok = pl.BlockSpec((1, tm, d), row)
    mods = pl.BlockSpec((1, rblk, d), mod)
    route = pl.BlockSpec((1, tm, LANES), row)
    tri = jnp.tril(jnp.ones((tm, tm), F32), -1).astype(BF16)
    return pl.pallas_call(
        _merge_kernel,
        grid=(nb, s // tm),
        in_specs=[tok, tok, tok, tok, tok, mods, mods, mods,
                  _const_spec((d, d)), _const_spec((d, d)), _const_spec((d, d)),
                  _const_spec((1, d)), _const_spec((d, LANES)), _const_spec((d, LANES)),
                  _const_spec((1, LANES)), _const_spec((tm, tm))],
        out_specs=[tok, pl.BlockSpec((1, tm * SUB, LANES), row), route, route, route,
                   pl.BlockSpec((1, LANES), lambda b, i: (0, 0))],
        out_shape=[jax.ShapeDtypeStruct((nb, s, d), F32),
                   jax.ShapeDtypeStruct((nb, s * SUB, LANES), F32),
                   jax.ShapeDtypeStruct((nb, s, LANES), jnp.int32),
                   jax.ShapeDtypeStruct((nb, s, LANES), F32),
                   jax.ShapeDtypeStruct((nb, s, LANES), jnp.int32),
                   jax.ShapeDtypeStruct((1, LANES), F32)],
        scratch_shapes=[pltpu.VMEM((1, LANES), F32)],
        compiler_params=pltpu.CompilerParams(dimension_semantics=("arbitrary", "arbitrary"),
                                             vmem_limit_bytes=VMEM_LIMIT),
        name="merge",
    )(mixed, o, ga, gb, x, g1, sc, sh, wa, wb, wo, n2g, rwh, rwl, rb, tri)


def _wprep_kernel(w1_ref, w2_ref, perm_ref, g_ref, l_ref, w2o_ref):
    for j in range(w1_ref.shape[-1] // MXU_DIM):
        z = jnp.dot(w1_ref[0, 0, :, j * MXU_DIM:(j + 1) * MXU_DIM].astype(BF16), perm_ref[...],
                    preferred_element_type=F32)
        g_ref[0, :, j * LANES:(j + 1) * LANES] = z[:, :LANES].astype(BF16)
        l_ref[0, :, j * LANES:(j + 1) * LANES] = z[:, LANES:].astype(BF16)
    w2o_ref[0] = w2_ref[0, 0].astype(BF16)


def _wprep(layer, w1, w2):
    _, n_e, d, f2 = w1.shape
    f = f2 // 2
    src = jnp.concatenate([jnp.arange(0, MXU_DIM, 2), jnp.arange(1, MXU_DIM, 2)])
    perm = (jnp.arange(MXU_DIM)[:, None] == src[None, :]).astype(BF16)
    return pl.pallas_call(
        _wprep_kernel,
        grid=(n_e,),
        in_specs=[pl.BlockSpec((1, 1, d, f2), lambda e: (layer, e, 0, 0)),
                  pl.BlockSpec((1, 1, f, d), lambda e: (layer, e, 0, 0)),
                  _const_spec((MXU_DIM, MXU_DIM))],
        out_specs=[pl.BlockSpec((1, d, f), lambda e: (e, 0, 0)),
                   pl.BlockSpec((1, d, f), lambda e: (e, 0, 0)),
                   pl.BlockSpec((1, f, d), lambda e: (e, 0, 0))],
        out_shape=[jax.ShapeDtypeStruct((n_e, d, f), BF16),
                   jax.ShapeDtypeStruct((n_e, d, f), BF16),
                   jax.ShapeDtypeStruct((n_e, f, d), BF16)],
        compiler_params=pltpu.CompilerParams(dimension_semantics=("parallel",),
                                             vmem_limit_bytes=VMEM_LIMIT),
        name="wprep",
    )(w1, w2, perm)


def _expert_kernel(be_ref, nu_ref, x_ref, w1g_ref, w1l_ref, b1g_ref, b1l_ref, w2_ref, b2_ref, y_ref):
    del be_ref
    tb = x_ref.shape[0] // SUB

    @pl.when(pl.program_id(0) < nu_ref[0])
    def _():
        x = jnp.concatenate([x_ref[pl.ds(j, tb, stride=SUB), :] for j in range(SUB)], axis=1).astype(BF16)
        glu = jnp.minimum(jnp.dot(x, w1g_ref[0], preferred_element_type=F32) + b1g_ref[0], SWIGLU_LIMIT)
        lin = jnp.clip(jnp.dot(x, w1l_ref[0], preferred_element_type=F32) + b1l_ref[0],
                       -SWIGLU_LIMIT, SWIGLU_LIMIT)
        act = glu * jax.nn.sigmoid(SWIGLU_ALPHA * glu) * (lin + 1.0)
        y = jnp.dot(act.astype(BF16), w2_ref[0], preferred_element_type=F32) + b2_ref[0]
        for j in range(SUB):
            y_ref[pl.ds(j, tb, stride=SUB), :] = y[:, j * LANES:(j + 1) * LANES]

    @pl.when(pl.program_id(0) >= nu_ref[0])
    def _():
        y_ref[...] = jnp.zeros(y_ref.shape, F32)


def _experts(block_expert, n_used, xs, w1g, w1l, b1g, b1l, w2, b2, *, tb):
    n_slots = xs.shape[0] // SUB
    n_e, d, f = w1g.shape
    emap = lambda i, be, nu: (be[i], 0, 0)
    grid_spec = pltpu.PrefetchScalarGridSpec(
        num_scalar_prefetch=2,
        grid=(n_slots // tb,),
        in_specs=[pl.BlockSpec((tb * SUB, LANES), lambda i, be, nu: (jnp.minimum(i, nu[0] - 1), 0)),
                  pl.BlockSpec((1, d, f), emap),
                  pl.BlockSpec((1, d, f), emap),
                  pl.BlockSpec((1, 1, f), emap),
                  pl.BlockSpec((1, 1, f), emap),
                  pl.BlockSpec((1, f, d), emap),
                  pl.BlockSpec((1, 1, d), emap)],
        out_specs=pl.BlockSpec((tb * SUB, LANES), lambda i, be, nu: (i, 0)))
    return pl.pallas_call(
        _expert_kernel,
        grid_spec=grid_spec,
        out_shape=jax.ShapeDtypeStruct((n_slots * SUB, LANES), F32),
        compiler_params=pltpu.CompilerParams(dimension_semantics=("arbitrary",),
                                             vmem_limit_bytes=VMEM_LIMIT),
        name="experts",
    )(block_expert, n_used, xs, w1g, w1l, b1g.reshape(n_e, 1, f), b1l.reshape(n_e, 1, f), w2,
      b2.reshape(n_e, 1, d))


def _dispatch_kernel(dest_ref, h_ref, xs_init_ref, xs_ref, sem):
    del xs_init_ref
    tm = h_ref.shape[0] // SUB

    def issue(t, carry):
        for k in range(TOP_K):
            slot = dest_ref[0, 0, t * TOP_K + k]
            pltpu.make_async_copy(h_ref.at[pl.ds(pl.multiple_of(t * SUB, SUB), SUB)],
                                  xs_ref.at[pl.ds(pl.multiple_of(slot * SUB, SUB), SUB)], sem).start()
        return carry

    lax.fori_loop(0, tm, issue, 0)
    for _ in range(TOP_K):
        pltpu.make_async_copy(h_ref, xs_ref.at[pl.ds(0, tm * SUB)], sem).wait()


def _dispatch(dest_tiles, h, n_slots):
    n_tiles, _, n = dest_tiles.shape
    tm = n // TOP_K
    return pl.pallas_call(
        _dispatch_kernel,
        grid=(n_tiles,),
        in_specs=[pl.BlockSpec((1, 1, n), lambda i: (i, 0, 0), memory_space=pltpu.SMEM),
                  pl.BlockSpec((tm * SUB, LANES), lambda i: (i, 0)),
                  pl.BlockSpec(memory_space=pl.ANY)],
        out_specs=pl.BlockSpec(memory_space=pl.ANY),
        out_shape=jax.ShapeDtypeStruct((n_slots * SUB, LANES), F32),
        scratch_shapes=[pltpu.SemaphoreType.DMA(())],
        input_output_aliases={2: 0},
        compiler_params=pltpu.CompilerParams(dimension_semantics=("arbitrary",)),
        name="dispatch",
    )(dest_tiles, h, jnp.zeros((n_slots * SUB, LANES), F32))


def _combine_kernel(dcur_ref, dnext_ref, gate_ref, x_ref, g2_ref, yb_ref, o_ref, buf_ref, sem):
    i = pl.program_id(0)
    tm = x_ref.shape[0]

    def gather(d_ref, slot):
        def issue(t, carry):
            for k in range(TOP_K):
                src = pl.multiple_of(d_ref[0, 0, t * TOP_K + k] * SUB, SUB)
                pltpu.make_async_copy(yb_ref.at[pl.ds(src, SUB)],
                                      buf_ref.at[slot, k, pl.ds(pl.multiple_of(t * SUB, SUB), SUB)],
                                      sem.at[slot]).start()
            return carry
        lax.fori_loop(0, tm, issue, 0)

    @pl.when(i == 0)
    def _():
        gather(dcur_ref, 0)

    slot = i % 2

    @pl.when(i + 1 < pl.num_programs(0))
    def _():
        gather(dnext_ref, 1 - slot)

    for k in range(TOP_K):
        pltpu.make_async_copy(yb_ref.at[pl.ds(0, tm * SUB)], buf_ref.at[slot, k], sem.at[slot]).wait()
    gates = gate_ref[...]
    for j in range(SUB):
        cs = slice(j * LANES, (j + 1) * LANES)
        y = buf_ref[slot, 0, pl.ds(j, tm, stride=SUB), :] * gates[:, 0:1]
        for k in range(1, TOP_K):
            y = y + buf_ref[slot, k, pl.ds(j, tm, stride=SUB), :] * gates[:, k:k + 1]
        o_ref[:, cs] = x_ref[:, cs] + g2_ref[0, :, cs] * y


def _combine(dest_tiles, gates, x, g2, yb, *, tiles_per_seq):
    t, d = x.shape
    n_tiles, _, n = dest_tiles.shape
    tm = n // TOP_K
    if g2.shape[1] == 1:
        g2_spec = pl.BlockSpec((1, 1, d), lambda i: (i // tiles_per_seq, 0, 0))
    else:
        g2_spec = pl.BlockSpec((1, tm, d), lambda i: (i // tiles_per_seq, i % tiles_per_seq, 0))
    dspec = lambda f: pl.BlockSpec((1, 1, n), f, memory_space=pltpu.SMEM)
    return pl.pallas_call(
        _combine_kernel,
        grid=(n_tiles,),
        in_specs=[dspec(lambda i: (i, 0, 0)),
                  dspec(lambda i: (jnp.minimum(i + 1, n_tiles - 1), 0, 0)),
                  pl.BlockSpec((tm, LANES), lambda i: (i, 0)),
                  pl.BlockSpec((tm, d), lambda i: (i, 0)),
                  g2_spec,
                  pl.BlockSpec(memory_space=pl.ANY)],
        out_specs=pl.BlockSpec((tm, d), lambda i: (i, 0)),
        out_shape=jax.ShapeDtypeStruct((t, d), F32),
        scratch_shapes=[pltpu.VMEM((2, TOP_K, tm * SUB, LANES), F32),
                        pltpu.SemaphoreType.DMA((2,))],
        compiler_params=pltpu.CompilerParams(dimension_semantics=("arbitrary",),
                                             vmem_limit_bytes=VMEM_LIMIT),
        name="combine",
    )(dest_tiles, dest_tiles, gates, x, g2, yb)


def _moe(h2, idx, gates, rank, cnt, x, g2, w1g, w1l, b1g, b1l, w2, b2, *, tb, tm):
    nb, s, d = x.shape
    t = nb * s
    n_assign = t * TOP_K
    counts = cnt[0, :N_EXPERTS].astype(jnp.int32)
    padded = (counts + tb - 1) // tb * tb
    pad_ends = jnp.cumsum(padded)
    pad_starts = pad_ends - padded
    idx4 = idx.reshape(t, LANES)[:, :TOP_K]
    rank4 = rank.reshape(t, LANES)[:, :TOP_K]
    experts = jnp.arange(N_EXPERTS, dtype=jnp.int32)
    dest = rank4 + jnp.sum(jnp.where(idx4[:, :, None] == experts, pad_starts, 0), axis=-1)
    dest_tiles = dest.reshape(t // tm, 1, tm * TOP_K)
    n_blocks = -(-n_assign // tb) + N_EXPERTS
    block_start = jnp.arange(n_blocks, dtype=jnp.int32) * tb
    block_expert = jnp.minimum(jnp.sum((block_start[:, None] >= pad_ends[None, :]).astype(jnp.int32), axis=1),
                               N_EXPERTS - 1)
    n_used = (pad_ends[-1:] // tb).astype(jnp.int32)
    xs = _dispatch(dest_tiles, h2.reshape(t * SUB, LANES), n_blocks * tb)
    yb = _experts(block_expert, n_used, xs, w1g, w1l, b1g, b1l, w2, b2, tb=tb)
    out = _combine(dest_tiles, gates.reshape(t, LANES), x.reshape(t, d), g2, yb, tiles_per_seq=s // tm)
    return out.reshape(nb, s, d)


def _lambda_init(layer):
    return 0.8 - 0.6 * math.exp(-0.3 * layer)


def _decode_bias(n_pos, visible):
    col = jnp.arange(n_pos * KV_HEADS, dtype=jnp.int32)[None, :]
    row = jnp.arange(2 * N_HEADS, dtype=jnp.int32)[:, None]
    ok = ((col // n_pos) == ((row % N_HEADS) // GQA_GROUP)) & ((col % n_pos) < visible)
    return jnp.where(ok, 0.0, NEG).astype(F32)


def kernel(x_prompt, x_sample, c_prompt, c_sample, cache_k, cache_v, page_table, ada_w, ada_b, norm1_g, norm2_g, w_in, q_norm_g, k_norm_g, lambda_q1, lambda_k1, lambda_q2, lambda_k2, subln_g, ln_v_g, ln_v_b, w_spatial, b_spatial, w_branch_a, w_branch_b, w_out, router_w, router_b, w1, b1, w2, b2):
    depth = ada_w.shape[0]
    nb, seq, d = x_prompt.shape
    db, ds, _ = x_sample.shape
    assert ds == 1, "sampled tokens open a fresh chunk one row at a time"
    assert d == SUB * LANES, "MoE rows are moved as one float32 tile per token"
    n_phys, page = cache_k.shape[1], cache_k.shape[2]

    mods = _ada_mods(jnp.concatenate([c_prompt, c_sample], axis=0), ada_w, ada_b)
    ck = jnp.transpose(cache_k, (0, 1, 3, 4, 5, 2))
    cv = cache_v.reshape(depth, n_phys, page * KV_HEADS, V_DIM)
    bias = _decode_bias(page, page)
    bias_new = _decode_bias(page, ds)
    tri = jnp.tril(jnp.ones((CHUNK, CHUNK), F32))

    yp = x_prompt
    ys = x_sample.reshape(1, db, d)
    kp_l, vp_l, ks_l, vs_l, cv_l = [], [], [], [], []
    for l in range(depth):
        lam_init = _lambda_init(l)
        lam = (jnp.exp(jnp.sum(lambda_q1[l] * lambda_k1[l])) - jnp.exp(jnp.sum(lambda_q2[l] * lambda_k2[l]))
               + lam_init).reshape(1).astype(F32)
        m6 = mods[l].reshape(nb + db, 6, d)
        mp = [m6[:nb, i][:, None, :] for i in range(6)]
        msm = [m6[nb:, i][None, :, :] for i in range(6)]
        w_in_b = w_in[l].astype(BF16)
        n1g = norm1_g[l][None, :]
        n2g = norm2_g[l][None, :]
        lng, lnb = ln_v_g[l][None, :], ln_v_b[l][None, :]
        qg = jnp.tile(q_norm_g[l], 2)[None, :]
        kg = jnp.tile(k_norm_g[l], 2)[None, :]
        sg = subln_g[l][None, :]
        sg_col = subln_g[l][:, None]
        ws_p = (w_spatial[l] * tri).astype(BF16)
        bs_p = jnp.repeat(b_spatial[l].T, LANES, axis=1)
        ws_s = jnp.repeat(w_spatial[l][:, 0, 0], LANES)[None, :]
        bs_s = jnp.repeat(b_spatial[l][:, 0], LANES)[None, :]
        wa, wb, wo = (w_branch_a[l].astype(BF16), w_branch_b[l].astype(BF16), w_out[l].astype(BF16))
        rw = jnp.pad(router_w[l], ((0, 0), (0, LANES - N_EXPERTS)))
        rwh, rwl = _split_bf16(rw)
        rb = jnp.pad(router_b[l], (0, LANES - N_EXPERTS), constant_values=NEG)[None, :]
        w1g, w1l, w2p = _wprep(l, w1, w2)
        b1g, b1l = b1[l][:, 0::2], b1[l][:, 1::2]
        ew = (w1g, w1l, b1g, b1l, w2p, b2[l])
        out_scale = 1.0 - lam_init

        mixed, q, k, v, kb, vb, ga, gb = _inproj(yp, mp[1], mp[0], n1g, w_in_b, lng, lnb, qg, kg,
                                                 ws_p, bs_p, sample=False, tm=TM_DENSE,
                                                 q_scale=HEAD_DIM ** -0.5 * LOG2E)
        o = _attn_prompt(lam, q, kb, vb, sg_col, out_scale=out_scale, tq=TQ, tk=TK)
        xo, h2, *route = _merge(mixed, o, ga, gb, yp, mp[2], mp[4], mp[3], wa, wb, wo, n2g, rwh, rwl, rb,
                                tm=TM_DENSE)
        yp = _moe(h2, *route, xo, mp[5], *ew, tb=TB_PROMPT, tm=TM_DENSE)
        kp_l.append(k.reshape(nb, seq, KV_HEADS, 2, HEAD_DIM))
        vp_l.append(v.reshape(nb, seq, KV_HEADS, V_DIM))

        mixed, q, k, v, kb, vb, ga, gb, cvs = _inproj(ys, msm[1], msm[0], n1g, w_in_b, lng, lnb, qg, kg,
                                                      ws_s, bs_s, sample=True, tm=db,
                                                      q_scale=HEAD_DIM ** -0.5)
        q16 = q.reshape(db, N_HEADS, 2, HEAD_DIM).transpose(0, 2, 1, 3).reshape(db, 2 * N_HEADS, HEAD_DIM)
        k_new = jnp.pad(k.reshape(db, KV_HEADS, 2, HEAD_DIM, 1), ((0, 0),) * 4 + ((0, page - 1),))
        v_new = jnp.pad(v.reshape(db, KV_HEADS, V_DIM), ((0, 0), (0, (page - 1) * KV_HEADS), (0, 0)))
        o = _attn_decode(l, page_table, lam, q16, bias, bias_new, k_new, v_new, sg, ck, cv,
                         out_scale=out_scale)
        o = o.reshape(1, db, d).astype(BF16)
        xo, h2, *route = _merge(mixed, o, ga, gb, ys, msm[2], msm[4], msm[3], wa, wb, wo, n2g, rwh, rwl, rb,
                                tm=db)
        ys = _moe(h2, *route, xo, msm[5], *ew, tb=TB_SAMPLE, tm=db)
        ks_l.append(k.reshape(db, ds, KV_HEADS, 2, HEAD_DIM))
        vs_l.append(v.reshape(db, ds, KV_HEADS, V_DIM))
        cv_l.append(cvs.reshape(db, ds, d))

    return (yp, ys.reshape(db, ds, d), jnp.stack(kp_l), jnp.stack(vp_l), jnp.stack(ks_l),
            jnp.stack(vs_l), jnp.stack(cv_l))
```

```python
import functools
import math

import jax
import jax.numpy as jnp
from jax import lax
from jax.experimental import pallas as pl
from jax.experimental.pallas import tpu as pltpu

F32 = jnp.float32
BF16 = jnp.bfloat16

LANES = 128
MXU_DIM = 256
HEAD_DIM = 64
N_HEADS = 8
KV_HEADS = 4
GQA_GROUP = N_HEADS // KV_HEADS
V_DIM = 2 * HEAD_DIM
CHUNK = 128
A_GROUPS = 8
N_EXPERTS = 32
TOP_K = 4
SUB = 8
SWIGLU_ALPHA = 1.702
SWIGLU_LIMIT = 7.0
LOG2E = math.log2(math.e)
NEG = -1e30

TM_DENSE = 256
TQ = 512
TK = 512
ONES_ROWS = 16
PAGES_PER_STEP = 16
TB_PROMPT = 256
TB_SAMPLE = 16
VMEM_LIMIT = 48 * 1024 * 1024


def _const_spec(shape):
    nd = len(shape)
    return pl.BlockSpec(shape, lambda *_: (0,) * nd, pipeline_mode=pl.Buffered(1))


def _gelu(x):
    return x * (lax.erf(x * (1.0 / math.sqrt(2.0))) + 1.0) * 0.5


def _rms(x, eps=1e-6):
    return x * lax.rsqrt(jnp.mean(x * x, axis=-1, keepdims=True) + eps)


def _ada_kernel(c_ref, w_ref, b_ref, o_ref):
    c = c_ref[...]
    sc = (c * jax.nn.sigmoid(c)).astype(BF16)
    o_ref[0] = jnp.dot(sc, w_ref[0].astype(BF16), preferred_element_type=F32) + b_ref[0]


def _ada_mods(c_all, ada_w, ada_b):
    depth, d, n = ada_w.shape
    r = c_all.shape[0]
    tn = 512
    return pl.pallas_call(
        _ada_kernel,
        grid=(depth, n // tn),
        in_specs=[pl.BlockSpec((r, d), lambda l, j: (0, 0)),
                  pl.BlockSpec((1, d, tn), lambda l, j: (l, 0, j)),
                  pl.BlockSpec((1, 1, tn), lambda l, j: (l, 0, j))],
        out_specs=pl.BlockSpec((1, r, tn), lambda l, j: (l, 0, j)),
        out_shape=jax.ShapeDtypeStruct((depth, r, n), F32),
        compiler_params=pltpu.CompilerParams(dimension_semantics=("parallel", "parallel")),
        name="ada",
    )(c_all, ada_w, ada_b.reshape(depth, 1, n))


def _half_rms(t, g_row, first):
    s = t * t
    s1 = jnp.sum(jnp.where(first, s, 0.0), axis=-1, keepdims=True)
    s2 = jnp.sum(jnp.where(first, 0.0, s), axis=-1, keepdims=True)
    r = jnp.where(first, lax.rsqrt(s1 * (1.0 / HEAD_DIM) + 1e-6),
                  lax.rsqrt(s2 * (1.0 / HEAD_DIM) + 1e-6))
    return t * r * g_row


def _inproj_kernel(x_ref, sc_ref, sh_ref, n1g_ref, w_ref, lng_ref, lnb_ref, qg_ref, kg_ref,
                   ws_ref, bs_ref, mixed_ref, q_ref, k_ref, v_ref, kb_ref, vb_ref, ga_ref, gb_ref,
                   *cv_ref, sample, q_scale):
    d = x_ref.shape[-1]
    tm = x_ref.shape[1]
    x = x_ref[0]
    h = (_rms(x) * n1g_ref[...] * (1.0 + sc_ref[0]) + sh_ref[0]).astype(BF16)

    def proj(lo, width):
        return jnp.dot(h, w_ref[:, lo:lo + width], preferred_element_type=F32)

    u = _gelu(proj(0, d))
    va = _gelu(proj(d, d))
    dv = va - jnp.mean(va, axis=-1, keepdims=True)
    van = dv * lax.rsqrt(jnp.mean(dv * dv, axis=-1, keepdims=True) + 1e-5) * lng_ref[...] + lnb_ref[...]
    if sample:
        cv_ref[0][0] = van
        mixed_ref[0] = (u * (van * ws_ref[...] + bs_ref[...])).astype(BF16)
    else:
        vb16 = van.astype(BF16)
        for c in range(tm // CHUNK):
            rs = slice(c * CHUNK, (c + 1) * CHUNK)
            for g in range(A_GROUPS):
                cs = slice(g * LANES, (g + 1) * LANES)
                mix = jnp.dot(ws_ref[g], vb16[rs, cs], preferred_element_type=F32) + bs_ref[:, cs]
                mixed_ref[0, rs, cs] = (u[rs, cs] * mix).astype(BF16)

    first = lax.broadcasted_iota(jnp.int32, (1, LANES), 1) < HEAD_DIM
    zq = proj(2 * d, N_HEADS * LANES)
    for hh in range(N_HEADS):
        cs = slice(hh * LANES, (hh + 1) * LANES)
        q_ref[0, :, cs] = (_half_rms(zq[:, cs], qg_ref[...], first) * q_scale).astype(BF16)
    zk = proj(2 * d + N_HEADS * LANES, KV_HEADS * LANES)
    for hh in range(KV_HEADS):
        cs = slice(hh * LANES, (hh + 1) * LANES)
        kn = _half_rms(zk[:, cs], kg_ref[...], first)
        if sample:
            k_ref[0, :, cs] = kn
        else:
            k_ref[0, cs, :] = kn.T
        kb_ref[0, :, cs] = kn.astype(BF16)
    zv = proj(2 * d + (N_HEADS + KV_HEADS) * LANES, KV_HEADS * LANES)
    if sample:
        v_ref[0] = zv
    else:
        for hh in range(KV_HEADS):
            v_ref[0, pl.ds(hh, tm, stride=KV_HEADS), :] = zv[:, hh * LANES:(hh + 1) * LANES]
    vb_ref[0] = zv.astype(BF16)
    base = 2 * d + (N_HEADS + 2 * KV_HEADS) * LANES
    ga_ref[0] = jax.nn.sigmoid(proj(base, d)).astype(BF16)
    gb_ref[0] = jax.nn.sigmoid(proj(base + d, d)).astype(BF16)


def _inproj(x, sc, sh, n1g, w_in, lng, lnb, qg, kg, ws, bs, *, sample, tm, q_scale):
    nb, s, d = x.shape
    r = sc.shape[1]
    kw = KV_HEADS * LANES
    row = lambda b, i: (b, i, 0)
    mod = (lambda b, i: (b, i, 0)) if r == s else (lambda b, i: (b, 0, 0))
    rblk = tm if r == s else 1
    in_specs = [pl.BlockSpec((1, tm, d), row),
                pl.BlockSpec((1, rblk, d), mod),
                pl.BlockSpec((1, rblk, d), mod),
                _const_spec((1, d)),
                _const_spec(w_in.shape),
                _const_spec((1, d)), _const_spec((1, d)),
                _const_spec((1, LANES)), _const_spec((1, LANES)),
                _const_spec(ws.shape), _const_spec(bs.shape)]
    out_shapes = [jax.ShapeDtypeStruct((nb, s, d), BF16),
                  jax.ShapeDtypeStruct((nb, s, d), BF16),
                  jax.ShapeDtypeStruct((nb, s, kw), F32),
                  jax.ShapeDtypeStruct((nb, s, kw), F32),
                  jax.ShapeDtypeStruct((nb, s, kw), BF16),
                  jax.ShapeDtypeStruct((nb, s, kw), BF16),
                  jax.ShapeDtypeStruct((nb, s, d), BF16),
                  jax.ShapeDtypeStruct((nb, s, d), BF16)]
    if sample:
        out_shapes.append(jax.ShapeDtypeStruct((nb, s, d), F32))
    out_specs = [pl.BlockSpec((1, tm, o.shape[-1]), row) for o in out_shapes]
    if not sample:
        out_shapes[2] = jax.ShapeDtypeStruct((nb, kw, s), F32)
        out_shapes[3] = jax.ShapeDtypeStruct((nb, s * KV_HEADS, V_DIM), F32)
        out_specs[2] = pl.BlockSpec((1, kw, tm), lambda b, i: (b, 0, i))
        out_specs[3] = pl.BlockSpec((1, tm * KV_HEADS, V_DIM), row)
    return pl.pallas_call(
        functools.partial(_inproj_kernel, sample=sample, q_scale=q_scale),
        grid=(nb, s // tm),
        in_specs=in_specs,
        out_specs=out_specs,
        out_shape=out_shapes,
        compiler_params=pltpu.CompilerParams(dimension_semantics=("parallel", "parallel"),
                                             vmem_limit_bytes=VMEM_LIMIT),
        name="inproj_sample" if sample else "inproj",
    )(x, sc, sh, n1g, w_in, lng, lnb, qg, kg, ws, bs)


def _attn_kernel(lam_ref, q_ref, k_ref, v_ref, sg_ref, o_ref, qs_ref, vt_ref, sa_ref, sb_ref, m_ref, acc_ref,
                 *, tq, tk, out_scale):
    qi = pl.program_id(2)
    n_kt = vt_ref.shape[0]

    @pl.when(qi == 0)
    def _():
        def fill(j, carry):
            start = pl.multiple_of(j * tk, tk)
            vt_ref[j, :LANES, :] = v_ref[0, pl.ds(start, tk), :].astype(F32).T.astype(BF16)
            vt_ref[j, LANES:, :] = jnp.ones((ONES_ROWS, tk), BF16)
            return carry
        lax.fori_loop(0, n_kt, fill, 0)

    first = lax.broadcasted_iota(jnp.int32, (LANES, 1), 0) < HEAD_DIM
    for g in range(GQA_GROUP):
        qt = q_ref[0, :, g * LANES:(g + 1) * LANES].astype(F32).T
        qs_ref[:, (2 * g) * tq:(2 * g + 1) * tq] = jnp.where(first, qt, 0.0).astype(BF16)
        qs_ref[:, (2 * g + 1) * tq:(2 * g + 2) * tq] = jnp.where(first, 0.0, qt).astype(BF16)
    m_ref[...] = jnp.full(m_ref.shape, NEG, F32)
    acc_ref[...] = jnp.zeros(acc_ref.shape, F32)

    def qk_into(s_ref, kj):
        start = pl.multiple_of(kj * tk, tk)
        s_ref[...] = jnp.dot(k_ref[0, pl.ds(start, tk), :], qs_ref[...], preferred_element_type=F32)

    def process(s_ref, kj, masked):
        s = s_ref[...]
        if masked:
            kpos = kj * tk + lax.broadcasted_iota(jnp.int32, s.shape, 0)
            qpos = qi * tq + (lax.broadcasted_iota(jnp.int32, s.shape, 1) & (tq - 1))
            s = jnp.where(kpos <= qpos, s, NEG)
        m_prev = m_ref[...]
        m_new = jnp.maximum(m_prev, jnp.max(s, axis=0, keepdims=True))
        alpha = jnp.exp2(m_prev - m_new)
        p = jnp.exp2(s - m_new)
        acc_ref[...] = alpha * acc_ref[...] + jnp.dot(vt_ref[kj], p.astype(BF16),
                                                      preferred_element_type=F32)
        m_ref[...] = m_new

    n_full = (qi * tq) // tk
    qk_into(sa_ref, 0)

    def pair(i, carry):
        qk_into(sb_ref, 2 * i + 1)
        process(sa_ref, 2 * i, False)
        qk_into(sa_ref, 2 * i + 2)
        process(sb_ref, 2 * i + 1, False)
        return carry

    lax.fori_loop(0, n_full // 2, pair, 0)

    @pl.when(n_full % 2 == 1)
    def _():
        qk_into(sb_ref, n_full)
        process(sa_ref, n_full - 1, False)
        process(sb_ref, n_full, True)

    @pl.when(n_full % 2 == 0)
    def _():
        process(sa_ref, n_full, True)

    o_all = acc_ref[:LANES, :] / acc_ref[LANES:LANES + 1, :]
    lam = lam_ref[0]
    for g in range(GQA_GROUP):
        og = o_all[:, (2 * g) * tq:(2 * g + 1) * tq] - lam * o_all[:, (2 * g + 1) * tq:(2 * g + 2) * tq]
        og = og * lax.rsqrt(jnp.mean(og * og, axis=0, keepdims=True) + 1e-6) * sg_ref[...] * out_scale
        o_ref[0, :, g * LANES:(g + 1) * LANES] = og.T.astype(BF16)


def _attn_prompt(lam, q, kb, vb, sg_col, *, out_scale, tq, tk):
    nb, s, d = q.shape
    gw = GQA_GROUP * LANES
    m = 2 * GQA_GROUP * tq
    return pl.pallas_call(
        functools.partial(_attn_kernel, tq=tq, tk=tk, out_scale=out_scale),
        grid=(nb, KV_HEADS, s // tq),
        in_specs=[pl.BlockSpec(memory_space=pltpu.SMEM),
                  pl.BlockSpec((1, tq, gw), lambda b, h, i: (b, i, h)),
                  pl.BlockSpec((1, s, LANES), lambda b, h, i: (b, 0, h)),
                  pl.BlockSpec((1, s, LANES), lambda b, h, i: (b, 0, h)),
                  _const_spec((LANES, 1))],
        out_specs=pl.BlockSpec((1, tq, gw), lambda b, h, i: (b, i, h)),
        out_shape=jax.ShapeDtypeStruct((nb, s, d), BF16),
        scratch_shapes=[pltpu.VMEM((LANES, m), BF16),
                        pltpu.VMEM((s // tk, LANES + ONES_ROWS, tk), BF16),
                        pltpu.VMEM((tk, m), F32),
                        pltpu.VMEM((tk, m), F32),
                        pltpu.VMEM((1, m), F32),
                        pltpu.VMEM((LANES + ONES_ROWS, m), F32)],
        compiler_params=pltpu.CompilerParams(
            dimension_semantics=("parallel", "parallel", "arbitrary"),
            vmem_limit_bytes=VMEM_LIMIT),
        name="attn_prompt",
    )(lam, q, kb, vb, sg_col)


def _decode_kernel(pt_ref, lam_ref, q_ref, bias_ref, biasn_ref, kn_ref, vn_ref, sg_ref, *rest,
                   n_pg, out_scale):
    k_refs = rest[:n_pg]
    v_refs = rest[n_pg:2 * n_pg]
    o_ref = rest[2 * n_pg]
    m_ref, l_ref, acc_ref = rest[2 * n_pg + 1:]
    del pt_ref
    j = pl.program_id(1)
    half = N_HEADS

    @pl.when(j == 0)
    def _():
        m_ref[...] = jnp.full(m_ref.shape, NEG, F32)
        l_ref[...] = jnp.zeros(l_ref.shape, F32)
        acc_ref[...] = jnp.zeros(acc_ref.shape, F32)

    q16 = q_ref[0]

    def scores(kref, idx, bias):
        parts = []
        for mp in range(2):
            kcat = jnp.concatenate([kref[idx + (kvh, mp)] for kvh in range(KV_HEADS)], axis=1)
            sm = jnp.dot(q16, kcat.astype(BF16), preferred_element_type=F32)
            parts.append(sm[mp * half:(mp + 1) * half])
        return jnp.concatenate(parts, axis=0) + bias

    def values(vref, idx):
        n = vref.shape[-2] // KV_HEADS
        return jnp.concatenate([vref[idx + (pl.ds(kvh, n, stride=KV_HEADS), slice(None))]
                                for kvh in range(KV_HEADS)], axis=0).astype(BF16)

    def update(s_list, v_list):
        s = jnp.concatenate(s_list, axis=1) if len(s_list) > 1 else s_list[0]
        m_prev = m_ref[...]
        m_new = jnp.maximum(m_prev, jnp.max(s, axis=-1, keepdims=True))
        alpha = jnp.exp(m_prev - m_new)
        p = jnp.exp(s - m_new)
        l_ref[...] = alpha * l_ref[...] + jnp.sum(p, axis=-1, keepdims=True)
        w = s_list[0].shape[1]
        pv = None
        for i, vv in enumerate(v_list):
            t = jnp.dot(p[:, i * w:(i + 1) * w].astype(BF16), vv, preferred_element_type=F32)
            pv = t if pv is None else pv + t
        acc_ref[...] = alpha * acc_ref[...] + pv
        m_ref[...] = m_new

    bias = bias_ref[...]
    update([scores(kr, (0, 0), bias) for kr in k_refs], [values(vr, (0, 0)) for vr in v_refs])

    @pl.when(j == pl.num_programs(1) - 1)
    def _():
        update([scores(kn_ref, (0,), biasn_ref[...])], [values(vn_ref, (0,))])
        o16 = acc_ref[...] / l_ref[...]
        og = o16[:half] - lam_ref[0] * o16[half:]
        o_ref[0] = _rms(og) * sg_ref[...] * out_scale


def _attn_decode(layer, page_table, lam, q16, bias, bias_new, k_new, v_new, sg, ck, cv, *, out_scale):
    db, n_pages = page_table.shape
    n_pg = PAGES_PER_STEP
    page = ck.shape[-1]
    rows_v = cv.shape[2]
    pt_flat = page_table.reshape(-1)

    def kmap(i):
        return lambda b, j, pt: (layer, pt[b * n_pages + j * n_pg + i], 0, 0, 0, 0)

    def vmap(i):
        return lambda b, j, pt: (layer, pt[b * n_pages + j * n_pg + i], 0, 0)

    per_b = lambda b, j, pt: (b, 0, 0)
    const2 = lambda b, j, pt: (0, 0)
    kblk = (KV_HEADS, 2, HEAD_DIM, page)
    in_specs = [pl.BlockSpec(memory_space=pltpu.SMEM),
                pl.BlockSpec((1, 2 * N_HEADS, HEAD_DIM), per_b),
                pl.BlockSpec(bias.shape, const2),
                pl.BlockSpec(bias_new.shape, const2),
                pl.BlockSpec((1,) + kblk, lambda b, j, pt: (b, 0, 0, 0, 0)),
                pl.BlockSpec((1, rows_v, V_DIM), per_b),
                pl.BlockSpec((1, LANES), const2)]
    in_specs += [pl.BlockSpec((1, 1) + kblk, kmap(i)) for i in range(n_pg)]
    in_specs += [pl.BlockSpec((1, 1, rows_v, V_DIM), vmap(i)) for i in range(n_pg)]
    grid_spec = pltpu.PrefetchScalarGridSpec(
        num_scalar_prefetch=1,
        grid=(db, n_pages // n_pg),
        in_specs=in_specs,
        out_specs=pl.BlockSpec((1, N_HEADS, V_DIM), per_b),
        scratch_shapes=[pltpu.VMEM((2 * N_HEADS, 1), F32),
                        pltpu.VMEM((2 * N_HEADS, 1), F32),
                        pltpu.VMEM((2 * N_HEADS, V_DIM), F32)])
    return pl.pallas_call(
        functools.partial(_decode_kernel, n_pg=n_pg, out_scale=out_scale),
        grid_spec=grid_spec,
        out_shape=jax.ShapeDtypeStruct((db, N_HEADS, V_DIM), F32),
        compiler_params=pltpu.CompilerParams(dimension_semantics=("parallel", "arbitrary"),
                                             vmem_limit_bytes=VMEM_LIMIT),
        name="attn_decode",
    )(pt_flat, lam, q16, bias, bias_new, k_new, v_new, sg, *([ck] * n_pg), *([cv] * n_pg))


def _split_bf16(x):
    hi = x.astype(BF16)
    return hi, (x - hi.astype(F32)).astype(BF16)


def _merge_kernel(mixed_ref, o_ref, ga_ref, gb_ref, x_ref, g1_ref, sc_ref, sh_ref, wa_ref, wb_ref,
                  wo_ref, n2g_ref, rwh_ref, rwl_ref, rb_ref, tri_ref,
                  xo_ref, h2_ref, idx_ref, gate_ref, rank_ref, cnt_ref, run_ref):
    @pl.when((pl.program_id(0) == 0) & (pl.program_id(1) == 0))
    def _():
        run_ref[...] = jnp.zeros(run_ref.shape, F32)

    a = jnp.dot(mixed_ref[0], wa_ref[...], preferred_element_type=F32)
    b = jnp.dot(o_ref[0], wb_ref[...], preferred_element_type=F32)
    merged = ga_ref[0].astype(F32) * a + gb_ref[0].astype(F32) * b
    xo = x_ref[0] + g1_ref[0] * jnp.dot(merged.astype(BF16), wo_ref[...], preferred_element_type=F32)
    xo_ref[0] = xo
    h2 = _rms(xo) * n2g_ref[...] * (1.0 + sc_ref[0]) + sh_ref[0]
    hi, lo = _split_bf16(h2)
    tm = hi.shape[0]
    hi32 = hi.astype(F32)
    for j in range(SUB):
        h2_ref[0, pl.ds(j, tm, stride=SUB), :] = hi32[:, j * LANES:(j + 1) * LANES]
    lg = (jnp.dot(hi, rwh_ref[...], preferred_element_type=F32)
          + jnp.dot(lo, rwh_ref[...], preferred_element_type=F32)
          + jnp.dot(hi, rwl_ref[...], preferred_element_type=F32)
          + rb_ref[...])

    lane = lax.broadcasted_iota(jnp.int32, lg.shape, 1)
    work = lg
    sels, vals, idxs = [], [], []
    for _ in range(TOP_K):
        mx = jnp.max(work, axis=-1, keepdims=True)
        ik = jnp.min(jnp.where(work == mx, lane, LANES), axis=-1, keepdims=True)
        sel = lane == ik
        work = jnp.where(sel, -jnp.inf, work)
        sels.append(sel)
        vals.append(mx)
        idxs.append(ik)
    es = [jnp.exp(v - vals[0]) for v in vals]
    den = es[0]
    for e in es[1:]:
        den = den + e

    onehot = sels[0].astype(F32)
    for sel in sels[1:]:
        onehot = onehot + sel.astype(F32)
    base = jnp.dot(tri_ref[...], onehot.astype(BF16), preferred_element_type=F32) + run_ref[...]
    idx_out = jnp.zeros(lg.shape, jnp.int32)
    gate_out = jnp.zeros(lg.shape, F32)
    rank_out = jnp.zeros(lg.shape, jnp.int32)
    for k in range(TOP_K):
        rk = jnp.sum(jnp.where(sels[k], base, 0.0), axis=-1, keepdims=True)
        idx_out = jnp.where(lane == k, idxs[k], idx_out)
        gate_out = jnp.where(lane == k, es[k] / den, gate_out)
        rank_out = jnp.where(lane == k, rk.astype(jnp.int32), rank_out)
    idx_ref[0] = idx_out
    gate_ref[0] = gate_out
    rank_ref[0] = rank_out
    run_ref[...] = run_ref[...] + jnp.sum(onehot, axis=0, keepdims=True)
    cnt_ref[...] = run_ref[...]


def _merge(mixed, o, ga, gb, x, g1, sc, sh, wa, wb, wo, n2g, rwh, rwl, rb, *, tm):
    nb, s, d = x.shape
    r = g1.shape[1]
    row = lambda b, i: (b, i, 0)
    mod = (lambda b, i: (b, i, 0)) if r == s else (lambda b, i: (b, 0, 0))
    rblk = tm if r == s else 1
    tok = pl.BlockSpec((1, tm, d), row)
    mods = pl.BlockSpec((1, rblk, d), mod)
    route = pl.BlockSpec((1, tm, LANES), row)
    tri = jnp.tril(jnp.ones((tm, tm), F32), -1).astype(BF16)
    return pl.pallas_call(
        _merge_kernel,
        grid=(nb, s // tm),
        in_specs=[tok, tok, tok, tok, tok, mods, mods, mods,
                  _const_spec((d, d)), _const_spec((d, d)), _const_spec((d, d)),
                  _const_spec((1, d)), _const_spec((d, LANES)), _const_spec((d, LANES)),
                  _const_spec((1, LANES)), _const_spec((tm, tm))],
        out_specs=[tok, pl.BlockSpec((1, tm * SUB, LANES), row), route, route, route,
                   pl.BlockSpec((1, LANES), lambda b, i: (0, 0))],
        out_shape=[jax.ShapeDtypeStruct((nb, s, d), F32),
                   jax.ShapeDtypeStruct((nb, s * SUB, LANES), F32),
                   jax.ShapeDtypeStruct((nb, s, LANES), jnp.int32),
                   jax.ShapeDtypeStruct((nb, s, LANES), F32),
                   jax.ShapeDtypeStruct((nb, s, LANES), jnp.int32),
                   jax.ShapeDtypeStruct((1, LANES), F32)],
        scratch_shapes=[pltpu.VMEM((1, LANES), F32)],
        compiler_params=pltpu.CompilerParams(dimension_semantics=("arbitrary", "arbitrary"),
                                             vmem_limit_bytes=VMEM_LIMIT),
        name="merge",
    )(mixed, o, ga, gb, x, g1, sc, sh, wa, wb, wo, n2g, rwh, rwl, rb, tri)


def _wprep_kernel(w1_ref, w2_ref, perm_ref, g_ref, l_ref, w2o_ref):
    for j in range(w1_ref.shape[-1] // MXU_DIM):
        z = jnp.dot(w1_ref[0, 0, :, j * MXU_DIM:(j + 1) * MXU_DIM].astype(BF16), perm_ref[...],
                    preferred_element_type=F32)
        g_ref[0, :, j * LANES:(j + 1) * LANES] = z[:, :LANES].astype(BF16)
        l_ref[0, :, j * LANES:(j + 1) * LANES] = z[:, LANES:].astype(BF16)
    w2o_ref[0] = w2_ref[0, 0].astype(BF16)


def _wprep(layer, w1, w2):
    _, n_e, d, f2 = w1.shape
    f = f2 // 2
    src = jnp.concatenate([jnp.arange(0, MXU_DIM, 2), jnp.arange(1, MXU_DIM, 2)])
    perm = (jnp.arange(MXU_DIM)[:, None] == src[None, :]).astype(BF16)
    return pl.pallas_call(
        _wprep_kernel,
        grid=(n_e,),
        in_specs=[pl.BlockSpec((1, 1, d, f2), lambda e: (layer, e, 0, 0)),
                  pl.BlockSpec((1, 1, f, d), lambda e: (layer, e, 0, 0)),
                  _const_spec((MXU_DIM, MXU_DIM))],
        out_specs=[pl.BlockSpec((1, d, f), lambda e: (e, 0, 0)),
                   pl.BlockSpec((1, d, f), lambda e: (e, 0, 0)),
                   pl.BlockSpec((1, f, d), lambda e: (e, 0, 0))],
        out_shape=[jax.ShapeDtypeStruct((n_e, d, f), BF16),
                   jax.ShapeDtypeStruct((n_e, d, f), BF16),
                   jax.ShapeDtypeStruct((n_e, f, d), BF16)],
        compiler_params=pltpu.CompilerParams(dimension_semantics=("parallel",),
                                             vmem_limit_bytes=VMEM_LIMIT),
        name="wprep",
    )(w1, w2, perm)


def _expert_kernel(be_ref, nu_ref, x_ref, w1g_ref, w1l_ref, b1g_ref, b1l_ref, w2_ref, b2_ref, y_ref):
    del be_ref
    tb = x_ref.shape[0] // SUB

    @pl.when(pl.program_id(0) < nu_ref[0])
    def _():
        x = jnp.concatenate([x_ref[pl.ds(j, tb, stride=SUB), :] for j in range(SUB)], axis=1).astype(BF16)
        glu = jnp.minimum(jnp.dot(x, w1g_ref[0], preferred_element_type=F32) + b1g_ref[0], SWIGLU_LIMIT)
        lin = jnp.clip(jnp.dot(x, w1l_ref[0], preferred_element_type=F32) + b1l_ref[0],
                       -SWIGLU_LIMIT, SWIGLU_LIMIT)
        act = glu * jax.nn.sigmoid(SWIGLU_ALPHA * glu) * (lin + 1.0)
        y = jnp.dot(act.astype(BF16), w2_ref[0], preferred_element_type=F32) + b2_ref[0]
        for j in range(SUB):
            y_ref[pl.ds(j, tb, stride=SUB), :] = y[:, j * LANES:(j + 1) * LANES]

    @pl.when(pl.program_id(0) >= nu_ref[0])
    def _():
        y_ref[...] = jnp.zeros(y_ref.shape, F32)


def _experts(block_expert, n_used, xs, w1g, w1l, b1g, b1l, w2, b2, *, tb):
    n_slots = xs.shape[0] // SUB
    n_e, d, f = w1g.shape
    emap = lambda i, be, nu: (be[i], 0, 0)
    grid_spec = pltpu.PrefetchScalarGridSpec(
        num_scalar_prefetch=2,
        grid=(n_slots // tb,),
        in_specs=[pl.BlockSpec((tb * SUB, LANES), lambda i, be, nu: (jnp.minimum(i, nu[0] - 1), 0)),
                  pl.BlockSpec((1, d, f), emap),
                  pl.BlockSpec((1, d, f), emap),
                  pl.BlockSpec((1, 1, f), emap),
                  pl.BlockSpec((1, 1, f), emap),
                  pl.BlockSpec((1, f, d), emap),
                  pl.BlockSpec((1, 1, d), emap)],
        out_specs=pl.BlockSpec((tb * SUB, LANES), lambda i, be, nu: (i, 0)))
    return pl.pallas_call(
        _expert_kernel,
        grid_spec=grid_spec,
        out_shape=jax.ShapeDtypeStruct((n_slots * SUB, LANES), F32),
        compiler_params=pltpu.CompilerParams(dimension_semantics=("arbitrary",),
                                             vmem_limit_bytes=VMEM_LIMIT),
        name="experts",
    )(block_expert, n_used, xs, w1g, w1l, b1g.reshape(n_e, 1, f), b1l.reshape(n_e, 1, f), w2,
      b2.reshape(n_e, 1, d))


def _dispatch_kernel(dest_ref, h_ref, xs_init_ref, xs_ref, sem):
    del xs_init_ref
    tm = h_ref.shape[0] // SUB

    def issue(t, carry):
        for k in range(TOP_K):
            slot = dest_ref[0, 0, t * TOP_K + k]
            pltpu.make_async_copy(h_ref.at[pl.ds(pl.multiple_of(t * SUB, SUB), SUB)],
                                  xs_ref.at[pl.ds(pl.multiple_of(slot * SUB, SUB), SUB)], sem).start()
        return carry

    lax.fori_loop(0, tm, issue, 0)
    for _ in range(TOP_K):
        pltpu.make_async_copy(h_ref, xs_ref.at[pl.ds(0, tm * SUB)], sem).wait()


def _dispatch(dest_tiles, h, n_slots, xs_init):
    if xs_init is None:
        xs_init = jnp.zeros((n_slots * SUB, LANES), F32)
    n_tiles, _, n = dest_tiles.shape
    tm = n // TOP_K
    return pl.pallas_call(
        _dispatch_kernel,
        grid=(n_tiles,),
        in_specs=[pl.BlockSpec((1, 1, n), lambda i: (i, 0, 0), memory_space=pltpu.SMEM),
                  pl.BlockSpec((tm * SUB, LANES), lambda i: (i, 0)),
                  pl.BlockSpec(memory_space=pl.ANY)],
        out_specs=pl.BlockSpec(memory_space=pl.ANY),
        out_shape=jax.ShapeDtypeStruct((n_slots * SUB, LANES), F32),
        scratch_shapes=[pltpu.SemaphoreType.DMA(())],
        input_output_aliases={2: 0},
        compiler_params=pltpu.CompilerParams(dimension_semantics=("arbitrary",)),
        name="dispatch",
    )(dest_tiles, h, xs_init)


def _combine_kernel(dcur_ref, dnext_ref, gate_ref, x_ref, g2_ref, yb_ref, o_ref, buf_ref, sem):
    i = pl.program_id(0)
    tm = x_ref.shape[0]

    def gather(d_ref, slot):
        def issue(t, carry):
            for k in range(TOP_K):
                src = pl.multiple_of(d_ref[0, 0, t * TOP_K + k] * SUB, SUB)
                pltpu.make_async_copy(yb_ref.at[pl.ds(src, SUB)],
                                      buf_ref.at[slot, k, pl.ds(pl.multiple_of(t * SUB, SUB), SUB)],
                                      sem.at[slot]).start()
            return carry
        lax.fori_loop(0, tm, issue, 0)

    @pl.when(i == 0)
    def _():
        gather(dcur_ref, 0)

    slot = i % 2

    @pl.when(i + 1 < pl.num_programs(0))
    def _():
        gather(dnext_ref, 1 - slot)

    for k in range(TOP_K):
        pltpu.make_async_copy(yb_ref.at[pl.ds(0, tm * SUB)], buf_ref.at[slot, k], sem.at[slot]).wait()
    gates = gate_ref[...]
    for j in range(SUB):
        cs = slice(j * LANES, (j + 1) * LANES)
        y = buf_ref[slot, 0, pl.ds(j, tm, stride=SUB), :] * gates[:, 0:1]
        for k in range(1, TOP_K):
            y = y + buf_ref[slot, k, pl.ds(j, tm, stride=SUB), :] * gates[:, k:k + 1]
        o_ref[:, cs] = x_ref[:, cs] + g2_ref[0, :, cs] * y


def _combine(dest_tiles, gates, x, g2, yb, *, tiles_per_seq):
    t, d = x.shape
    n_tiles, _, n = dest_tiles.shape
    tm = n // TOP_K
    if g2.shape[1] == 1:
        g2_spec = pl.BlockSpec((1, 1, d), lambda i: (i // tiles_per_seq, 0, 0))
    else:
        g2_spec = pl.BlockSpec((1, tm, d), lambda i: (i // tiles_per_seq, i % tiles_per_seq, 0))
    dspec = lambda f: pl.BlockSpec((1, 1, n), f, memory_space=pltpu.SMEM)
    return pl.pallas_call(
        _combine_kernel,
        grid=(n_tiles,),
        in_specs=[dspec(lambda i: (i, 0, 0)),
                  dspec(lambda i: (jnp.minimum(i + 1, n_tiles - 1), 0, 0)),
                  pl.BlockSpec((tm, LANES), lambda i: (i, 0)),
                  pl.BlockSpec((tm, d), lambda i: (i, 0)),
                  g2_spec,
                  pl.BlockSpec(memory_space=pl.ANY)],
        out_specs=pl.BlockSpec((tm, d), lambda i: (i, 0)),
        out_shape=jax.ShapeDtypeStruct((t, d), F32),
        scratch_shapes=[pltpu.VMEM((2, TOP_K, tm * SUB, LANES), F32),
                        pltpu.SemaphoreType.DMA((2,))],
        compiler_params=pltpu.CompilerParams(dimension_semantics=("arbitrary",),
                                             vmem_limit_bytes=VMEM_LIMIT),
        name="combine",
    )(dest_tiles, dest_tiles, gates, x, g2, yb)


def _moe(h2, idx, gates, rank, cnt, x, g2, w1g, w1l, b1g, b1l, w2, b2, *, tb, tm, xs_init=None):
    nb, s, d = x.shape
    t = nb * s
    n_assign = t * TOP_K
    counts = cnt[0, :N_EXPERTS].astype(jnp.int32)
    padded = (counts + tb - 1) // tb * tb
    pad_ends = jnp.cumsum(padded)
    pad_starts = pad_ends - padded
    idx4 = idx.reshape(t, LANES)[:, :TOP_K]
    rank4 = rank.reshape(t, LANES)[:, :TOP_K]
    experts = jnp.arange(N_EXPERTS, dtype=jnp.int32)
    dest = rank4 + jnp.sum(jnp.where(idx4[:, :, None] == experts, pad_starts, 0), axis=-1)
    dest_tiles = dest.reshape(t // tm, 1, tm * TOP_K)
    n_blocks = -(-n_assign // tb) + N_EXPERTS
    block_start = jnp.arange(n_blocks, dtype=jnp.int32) * tb
    block_expert = jnp.minimum(jnp.sum((block_start[:, None] >= pad_ends[None, :]).astype(jnp.int32), axis=1),
                               N_EXPERTS - 1)
    n_used = (pad_ends[-1:] // tb).astype(jnp.int32)
    xs = _dispatch(dest_tiles, h2.reshape(t * SUB, LANES), n_blocks * tb, xs_init)
    yb = _experts(block_expert, n_used, xs, w1g, w1l, b1g, b1l, w2, b2, tb=tb)
    out = _combine(dest_tiles, gates.reshape(t, LANES), x.reshape(t, d), g2, yb, tiles_per_seq=s // tm)
    return out.reshape(nb, s, d), yb


def _lambda_init(layer):
    return 0.8 - 0.6 * math.exp(-0.3 * layer)


def _decode_bias(n_pos, visible):
    col = jnp.arange(n_pos * KV_HEADS, dtype=jnp.int32)[None, :]
    row = jnp.arange(2 * N_HEADS, dtype=jnp.int32)[:, None]
    ok = ((col // n_pos) == ((row % N_HEADS) // GQA_GROUP)) & ((col % n_pos) < visible)
    return jnp.where(ok, 0.0, NEG).astype(F32)


def kernel(x_prompt, x_sample, c_prompt, c_sample, cache_k, cache_v, page_table, ada_w, ada_b, norm1_g, norm2_g, w_in, q_norm_g, k_norm_g, lambda_q1, lambda_k1, lambda_q2, lambda_k2, subln_g, ln_v_g, ln_v_b, w_spatial, b_spatial, w_branch_a, w_branch_b, w_out, router_w, router_b, w1, b1, w2, b2):
    depth = ada_w.shape[0]
    nb, seq, d = x_prompt.shape
    db, ds, _ = x_sample.shape
    assert ds == 1, "sampled tokens open a fresh chunk one row at a time"
    assert d == SUB * LANES, "MoE rows are moved as one float32 tile per token"
    n_phys, page = cache_k.shape[1], cache_k.shape[2]

    mods = _ada_mods(jnp.concatenate([c_prompt, c_sample], axis=0), ada_w, ada_b)
    ck = jnp.transpose(cache_k, (0, 1, 3, 4, 5, 2))
    cv = cache_v.reshape(depth, n_phys, page * KV_HEADS, V_DIM)
    bias = _decode_bias(page, page)
    bias_new = _decode_bias(page, ds)
    tri = jnp.tril(jnp.ones((CHUNK, CHUNK), F32))

    yp = x_prompt
    ys = x_sample.reshape(1, db, d)
    kp_l, vp_l, ks_l, vs_l, cv_l = [], [], [], [], []
    slots_p = slots_s = None
    for l in range(depth):
        lam_init = _lambda_init(l)
        lam = (jnp.exp(jnp.sum(lambda_q1[l] * lambda_k1[l])) - jnp.exp(jnp.sum(lambda_q2[l] * lambda_k2[l]))
               + lam_init).reshape(1).astype(F32)
        m6 = mods[l].reshape(nb + db, 6, d)
        mp = [m6[:nb, i][:, None, :] for i in range(6)]
        msm = [m6[nb:, i][None, :, :] for i in range(6)]
        w_in_b = w_in[l].astype(BF16)
        n1g = norm1_g[l][None, :]
        n2g = norm2_g[l][None, :]
        lng, lnb = ln_v_g[l][None, :], ln_v_b[l][None, :]
        qg = jnp.tile(q_norm_g[l], 2)[None, :]
        kg = jnp.tile(k_norm_g[l], 2)[None, :]
        sg = subln_g[l][None, :]
        sg_col = subln_g[l][:, None]
        ws_p = (w_spatial[l] * tri).astype(BF16)
        bs_p = jnp.repeat(b_spatial[l].T, LANES, axis=1)
        ws_s = jnp.repeat(w_spatial[l][:, 0, 0], LANES)[None, :]
        bs_s = jnp.repeat(b_spatial[l][:, 0], LANES)[None, :]
        wa, wb, wo = (w_branch_a[l].astype(BF16), w_branch_b[l].astype(BF16), w_out[l].astype(BF16))
        rw = jnp.pad(router_w[l], ((0, 0), (0, LANES - N_EXPERTS)))
        rwh, rwl = _split_bf16(rw)
        rb = jnp.pad(router_b[l], (0, LANES - N_EXPERTS), constant_values=NEG)[None, :]
        w1g, w1l, w2p = _wprep(l, w1, w2)
        b1g, b1l = b1[l][:, 0::2], b1[l][:, 1::2]
        ew = (w1g, w1l, b1g, b1l, w2p, b2[l])
        out_scale = 1.0 - lam_init

        mixed, q, k, v, kb, vb, ga, gb = _inproj(yp, mp[1], mp[0], n1g, w_in_b, lng, lnb, qg, kg,
                                                 ws_p, bs_p, sample=False, tm=TM_DENSE,
                                                 q_scale=HEAD_DIM ** -0.5 * LOG2E)
        o = _attn_prompt(lam, q, kb, vb, sg_col, out_scale=out_scale, tq=TQ, tk=TK)
        xo, h2, *route = _merge(mixed, o, ga, gb, yp, mp[2], mp[4], mp[3], wa, wb, wo, n2g, rwh, rwl, rb,
                                tm=TM_DENSE)
        yp, slots_p = _moe(h2, *route, xo, mp[5], *ew, tb=TB_PROMPT, tm=TM_DENSE, xs_init=slots_p)
        kp_l.append(k.reshape(nb, KV_HEADS, 2, HEAD_DIM, seq).transpose(0, 4, 1, 2, 3))
        vp_l.append(v.reshape(nb, seq, KV_HEADS, V_DIM))

        mixed, q, k, v, kb, vb, ga, gb, cvs = _inproj(ys, msm[1], msm[0], n1g, w_in_b, lng, lnb, qg, kg,
                                                      ws_s, bs_s, sample=True, tm=db,
                                                      q_scale=HEAD_DIM ** -0.5)
        q16 = q.reshape(db, N_HEADS, 2, HEAD_DIM).transpose(0, 2, 1, 3).reshape(db, 2 * N_HEADS, HEAD_DIM)
        k_new = jnp.pad(k.reshape(db, KV_HEADS, 2, HEAD_DIM, 1), ((0, 0),) * 4 + ((0, page - 1),))
        v_new = jnp.pad(v.reshape(db, KV_HEADS, V_DIM), ((0, 0), (0, (page - 1) * KV_HEADS), (0, 0)))
        o = _attn_decode(l, page_table, lam, q16, bias, bias_new, k_new, v_new, sg, ck, cv,
                         out_scale=out_scale)
        o = o.reshape(1, db, d).astype(BF16)
        xo, h2, *route = _merge(mixed, o, ga, gb, ys, msm[2], msm[4], msm[3], wa, wb, wo, n2g, rwh, rwl, rb,
                                tm=db)
        ys, slots_s = _moe(h2, *route, xo, msm[5], *ew, tb=TB_SAMPLE, tm=db, xs_init=slots_s)
        ks_l.append(k.reshape(db, ds, KV_HEADS, 2, HEAD_DIM))
        vs_l.append(v.reshape(db, ds, KV_HEADS, V_DIM))
        cv_l.append(cvs.reshape(db, ds, d))

    return (yp, ys.reshape(db, ds, d), jnp.stack(kp_l), jnp.stack(vp_l), jnp.stack(ks_l),
            jnp.stack(vs_l), jnp.stack(cv_l))
```

```python
import functools
import math

import jax
import jax.numpy as jnp
from jax import lax
from jax.experimental import pallas as pl
from jax.experimental.pallas import tpu as pltpu

F32 = jnp.float32
BF16 = jnp.bfloat16

LANES = 128
MXU_DIM = 256
HEAD_DIM = 64
N_HEADS = 8
KV_HEADS = 4
GQA_GROUP = N_HEADS // KV_HEADS
V_DIM = 2 * HEAD_DIM
CHUNK = 128
A_GROUPS = 8
N_EXPERTS = 32
TOP_K = 4
SUB = 8
SWIGLU_ALPHA = 1.702
SWIGLU_LIMIT = 7.0
LOG2E = math.log2(math.e)
NEG = -1e30

TM_DENSE = 512
TQ = 512
TK = 512
ONES_ROWS = 16
PAGES_PER_STEP = 16
TB_PROMPT = 256
TB_SAMPLE = 16
VMEM_LIMIT = 48 * 1024 * 1024


def _const_spec(shape):
    nd = len(shape)
    return pl.BlockSpec(shape, lambda *_: (0,) * nd, pipeline_mode=pl.Buffered(1))


def _gelu(x):
    return x * (lax.erf(x * (1.0 / math.sqrt(2.0))) + 1.0) * 0.5


def _rms(x, eps=1e-6):
    return x * lax.rsqrt(jnp.mean(x * x, axis=-1, keepdims=True) + eps)


def _ada_kernel(c_ref, w_ref, b_ref, o_ref):
    c = c_ref[...]
    sc = (c * jax.nn.sigmoid(c)).astype(BF16)
    o_ref[0] = jnp.dot(sc, w_ref[0].astype(BF16), preferred_element_type=F32) + b_ref[0]


def _ada_mods(c_all, ada_w, ada_b):
    depth, d, n = ada_w.shape
    r = c_all.shape[0]
    tn = 512
    return pl.pallas_call(
        _ada_kernel,
        grid=(depth, n // tn),
        in_specs=[pl.BlockSpec((r, d), lambda l, j: (0, 0)),
                  pl.BlockSpec((1, d, tn), lambda l, j: (l, 0, j)),
                  pl.BlockSpec((1, 1, tn), lambda l, j: (l, 0, j))],
        out_specs=pl.BlockSpec((1, r, tn), lambda l, j: (l, 0, j)),
        out_shape=jax.ShapeDtypeStruct((depth, r, n), F32),
        compiler_params=pltpu.CompilerParams(dimension_semantics=("parallel", "parallel")),
        name="ada",
    )(c_all, ada_w, ada_b.reshape(depth, 1, n))


def _half_rms(t, g_row, first):
    s = t * t
    s1 = jnp.sum(jnp.where(first, s, 0.0), axis=-1, keepdims=True)
    s2 = jnp.sum(jnp.where(first, 0.0, s), axis=-1, keepdims=True)
    r = jnp.where(first, lax.rsqrt(s1 * (1.0 / HEAD_DIM) + 1e-6),
                  lax.rsqrt(s2 * (1.0 / HEAD_DIM) + 1e-6))
    return t * r * g_row


def _inproj_kernel(x_ref, sc_ref, sh_ref, n1g_ref, w_ref, lng_ref, lnb_ref, qg_ref, kg_ref,
                   ws_ref, bs_ref, mixed_ref, q_ref, k_ref, v_ref, kb_ref, vb_ref, ga_ref, gb_ref,
                   *cv_ref, sample, q_scale):
    d = x_ref.shape[-1]
    tm = x_ref.shape[1]
    x = x_ref[0]
    h = (_rms(x) * n1g_ref[...] * (1.0 + sc_ref[0]) + sh_ref[0]).astype(BF16)

    def proj(lo, width):
        return jnp.dot(h, w_ref[:, lo:lo + width], preferred_element_type=F32)

    u = _gelu(proj(0, d))
    va = _gelu(proj(d, d))
    dv = va - jnp.mean(va, axis=-1, keepdims=True)
    van = dv * lax.rsqrt(jnp.mean(dv * dv, axis=-1, keepdims=True) + 1e-5) * lng_ref[...] + lnb_ref[...]
    if sample:
        cv_ref[0][0] = van
        mixed_ref[0] = (u * (van * ws_ref[...] + bs_ref[...])).astype(BF16)
    else:
        vb16 = van.astype(BF16)
        for c in range(tm // CHUNK):
            rs = slice(c * CHUNK, (c + 1) * CHUNK)
            for g in range(A_GROUPS):
                cs = slice(g * LANES, (g + 1) * LANES)
                mix = jnp.dot(ws_ref[g], vb16[rs, cs], preferred_element_type=F32) + bs_ref[:, cs]
                mixed_ref[0, rs, cs] = (u[rs, cs] * mix).astype(BF16)

    first = lax.broadcasted_iota(jnp.int32, (1, LANES), 1) < HEAD_DIM
    zq = proj(2 * d, N_HEADS * LANES)
    for hh in range(N_HEADS):
        cs = slice(hh * LANES, (hh + 1) * LANES)
        q_ref[0, :, cs] = (_half_rms(zq[:, cs], qg_ref[...], first) * q_scale).astype(BF16)
    zk = proj(2 * d + N_HEADS * LANES, KV_HEADS * LANES)
    for hh in range(KV_HEADS):
        cs = slice(hh * LANES, (hh + 1) * LANES)
        kn = _half_rms(zk[:, cs], kg_ref[...], first)
        if sample:
            k_ref[0, :, cs] = kn
        else:
            k_ref[0, cs, :] = kn.T
        kb_ref[0, :, cs] = kn.astype(BF16)
    zv = proj(2 * d + (N_HEADS + KV_HEADS) * LANES, KV_HEADS * LANES)
    if sample:
        v_ref[0] = zv
    else:
        for hh in range(KV_HEADS):
            v_ref[0, pl.ds(hh, tm, stride=KV_HEADS), :] = zv[:, hh * LANES:(hh + 1) * LANES]
    vb_ref[0] = zv.astype(BF16)
    base = 2 * d + (N_HEADS + 2 * KV_HEADS) * LANES
    ga_ref[0] = jax.nn.sigmoid(proj(base, d)).astype(BF16)
    gb_ref[0] = jax.nn.sigmoid(proj(base + d, d)).astype(BF16)


def _inproj(x, sc, sh, n1g, w_in, lng, lnb, qg, kg, ws, bs, *, sample, tm, q_scale):
    nb, s, d = x.shape
    r = sc.shape[1]
    kw = KV_HEADS * LANES
    row = lambda b, i: (b, i, 0)
    mod = (lambda b, i: (b, i, 0)) if r == s else (lambda b, i: (b, 0, 0))
    rblk = tm if r == s else 1
    in_specs = [pl.BlockSpec((1, tm, d), row),
                pl.BlockSpec((1, rblk, d), mod),
                pl.BlockSpec((1, rblk, d), mod),
                _const_spec((1, d)),
                _const_spec(w_in.shape),
                _const_spec((1, d)), _const_spec((1, d)),
                _const_spec((1, LANES)), _const_spec((1, LANES)),
                _const_spec(ws.shape), _const_spec(bs.shape)]
    out_shapes = [jax.ShapeDtypeStruct((nb, s, d), BF16),
                  jax.ShapeDtypeStruct((nb, s, d), BF16),
                  jax.ShapeDtypeStruct((nb, s, kw), F32),
                  jax.ShapeDtypeStruct((nb, s, kw), F32),
                  jax.ShapeDtypeStruct((nb, s, kw), BF16),
                  jax.ShapeDtypeStruct((nb, s, kw), BF16),
                  jax.ShapeDtypeStruct((nb, s, d), BF16),
                  jax.ShapeDtypeStruct((nb, s, d), BF16)]
    if sample:
        out_shapes.append(jax.ShapeDtypeStruct((nb, s, d), F32))
    out_specs = [pl.BlockSpec((1, tm, o.shape[-1]), row) for o in out_shapes]
    if not sample:
        out_shapes[2] = jax.ShapeDtypeStruct((nb, kw, s), F32)
        out_shapes[3] = jax.ShapeDtypeStruct((nb, s * KV_HEADS, V_DIM), F32)
        out_specs[2] = pl.BlockSpec((1, kw, tm), lambda b, i: (b, 0, i))
        out_specs[3] = pl.BlockSpec((1, tm * KV_HEADS, V_DIM), row)
    return pl.pallas_call(
        functools.partial(_inproj_kernel, sample=sample, q_scale=q_scale),
        grid=(nb, s // tm),
        in_specs=in_specs,
        out_specs=out_specs,
        out_shape=out_shapes,
        compiler_params=pltpu.CompilerParams(dimension_semantics=("parallel", "parallel"),
                                             vmem_limit_bytes=VMEM_LIMIT),
        name="inproj_sample" if sample else "inproj",
    )(x, sc, sh, n1g, w_in, lng, lnb, qg, kg, ws, bs)


def _attn_kernel(lam_ref, q_ref, k_ref, v_ref, sg_ref, o_ref, qs_ref, vt_ref, sa_ref, sb_ref, m_ref, acc_ref,
                 *, tq, tk, out_scale):
    qi = pl.program_id(2)
    n_kt = vt_ref.shape[0]

    @pl.when(qi == 0)
    def _():
        def fill(j, carry):
            start = pl.multiple_of(j * tk, tk)
            vt_ref[j, :LANES, :] = v_ref[0, pl.ds(start, tk), :].astype(F32).T.astype(BF16)
            vt_ref[j, LANES:, :] = jnp.ones((ONES_ROWS, tk), BF16)
            return carry
        lax.fori_loop(0, n_kt, fill, 0)

    first = lax.broadcasted_iota(jnp.int32, (LANES, 1), 0) < HEAD_DIM
    for g in range(GQA_GROUP):
        qt = q_ref[0, :, g * LANES:(g + 1) * LANES].astype(F32).T
        qs_ref[:, (2 * g) * tq:(2 * g + 1) * tq] = jnp.where(first, qt, 0.0).astype(BF16)
        qs_ref[:, (2 * g + 1) * tq:(2 * g + 2) * tq] = jnp.where(first, 0.0, qt).astype(BF16)
    m_ref[...] = jnp.full(m_ref.shape, NEG, F32)
    acc_ref[...] = jnp.zeros(acc_ref.shape, F32)

    def qk_into(s_ref, kj):
        start = pl.multiple_of(kj * tk, tk)
        s_ref[...] = jnp.dot(k_ref[0, pl.ds(start, tk), :], qs_ref[...], preferred_element_type=F32)

    def process(s_ref, kj, masked):
        s = s_ref[...]
        if masked:
            kpos = kj * tk + lax.broadcasted_iota(jnp.int32, s.shape, 0)
            qpos = qi * tq + (lax.broadcasted_iota(jnp.int32, s.shape, 1) & (tq - 1))
            s = jnp.where(kpos <= qpos, s, NEG)
        m_prev = m_ref[...]
        m_new = jnp.maximum(m_prev, jnp.max(s, axis=0, keepdims=True))
        alpha = jnp.exp2(m_prev - m_new)
        p = jnp.exp2(s - m_new)
        acc_ref[...] = alpha * acc_ref[...] + jnp.dot(vt_ref[kj], p.astype(BF16),
                                                      preferred_element_type=F32)
        m_ref[...] = m_new

    n_full = (qi * tq) // tk
    qk_into(sa_ref, 0)

    def pair(i, carry):
        qk_into(sb_ref, 2 * i + 1)
        process(sa_ref, 2 * i, False)
        qk_into(sa_ref, 2 * i + 2)
        process(sb_ref, 2 * i + 1, False)
        return carry

    lax.fori_loop(0, n_full // 2, pair, 0)

    @pl.when(n_full % 2 == 1)
    def _():
        qk_into(sb_ref, n_full)
        process(sa_ref, n_full - 1, False)
        process(sb_ref, n_full, True)

    @pl.when(n_full % 2 == 0)
    def _():
        process(sa_ref, n_full, True)

    o_all = acc_ref[:LANES, :] / acc_ref[LANES:LANES + 1, :]
    lam = lam_ref[0]
    for g in range(GQA_GROUP):
        og = o_all[:, (2 * g) * tq:(2 * g + 1) * tq] - lam * o_all[:, (2 * g + 1) * tq:(2 * g + 2) * tq]
        og = og * lax.rsqrt(jnp.mean(og * og, axis=0, keepdims=True) + 1e-6) * sg_ref[...] * out_scale
        o_ref[0, :, g * LANES:(g + 1) * LANES] = og.T.astype(BF16)


def _attn_prompt(lam, q, kb, vb, sg_col, *, out_scale, tq, tk):
    nb, s, d = q.shape
    gw = GQA_GROUP * LANES
    m = 2 * GQA_GROUP * tq
    return pl.pallas_call(
        functools.partial(_attn_kernel, tq=tq, tk=tk, out_scale=out_scale),
        grid=(nb, KV_HEADS, s // tq),
        in_specs=[pl.BlockSpec(memory_space=pltpu.SMEM),
                  pl.BlockSpec((1, tq, gw), lambda b, h, i: (b, i, h)),
                  pl.BlockSpec((1, s, LANES), lambda b, h, i: (b, 0, h)),
                  pl.BlockSpec((1, s, LANES), lambda b, h, i: (b, 0, h)),
                  _const_spec((LANES, 1))],
        out_specs=pl.BlockSpec((1, tq, gw), lambda b, h, i: (b, i, h)),
        out_shape=jax.ShapeDtypeStruct((nb, s, d), BF16),
        scratch_shapes=[pltpu.VMEM((LANES, m), BF16),
                        pltpu.VMEM((s // tk, LANES + ONES_ROWS, tk), BF16),
                        pltpu.VMEM((tk, m), F32),
                        pltpu.VMEM((tk, m), F32),
                        pltpu.VMEM((1, m), F32),
                        pltpu.VMEM((LANES + ONES_ROWS, m), F32)],
        compiler_params=pltpu.CompilerParams(
            dimension_semantics=("parallel", "parallel", "arbitrary"),
            vmem_limit_bytes=VMEM_LIMIT),
        name="attn_prompt",
    )(lam, q, kb, vb, sg_col)


def _decode_kernel(pt_ref, lam_ref, q_ref, bias_ref, biasn_ref, kn_ref, vn_ref, sg_ref, *rest,
                   n_pg, out_scale):
    k_refs = rest[:n_pg]
    v_refs = rest[n_pg:2 * n_pg]
    o_ref = rest[2 * n_pg]
    m_ref, l_ref, acc_ref = rest[2 * n_pg + 1:]
    del pt_ref
    j = pl.program_id(1)
    half = N_HEADS

    @pl.when(j == 0)
    def _():
        m_ref[...] = jnp.full(m_ref.shape, NEG, F32)
        l_ref[...] = jnp.zeros(l_ref.shape, F32)
        acc_ref[...] = jnp.zeros(acc_ref.shape, F32)

    q16 = q_ref[0]

    def scores(kref, idx, bias):
        parts = []
        for mp in range(2):
            kcat = jnp.concatenate([kref[idx + (kvh, mp)] for kvh in range(KV_HEADS)], axis=1)
            sm = jnp.dot(q16, kcat.astype(BF16), preferred_element_type=F32)
            parts.append(sm[mp * half:(mp + 1) * half])
        return jnp.concatenate(parts, axis=0) + bias

    def values(vref, idx):
        n = vref.shape[-2] // KV_HEADS
        return jnp.concatenate([vref[idx + (pl.ds(kvh, n, stride=KV_HEADS), slice(None))]
                                for kvh in range(KV_HEADS)], axis=0).astype(BF16)

    def update(s_list, v_list):
        s = jnp.concatenate(s_list, axis=1) if len(s_list) > 1 else s_list[0]
        m_prev = m_ref[...]
        m_new = jnp.maximum(m_prev, jnp.max(s, axis=-1, keepdims=True))
        alpha = jnp.exp(m_prev - m_new)
        p = jnp.exp(s - m_new)
        l_ref[...] = alpha * l_ref[...] + jnp.sum(p, axis=-1, keepdims=True)
        w = s_list[0].shape[1]
        pv = None
        for i, vv in enumerate(v_list):
            t = jnp.dot(p[:, i * w:(i + 1) * w].astype(BF16), vv, preferred_element_type=F32)
            pv = t if pv is None else pv + t
        acc_ref[...] = alpha * acc_ref[...] + pv
        m_ref[...] = m_new

    bias = bias_ref[...]
    update([scores(kr, (0, 0), bias) for kr in k_refs], [values(vr, (0, 0)) for vr in v_refs])

    @pl.when(j == pl.num_programs(1) - 1)
    def _():
        update([scores(kn_ref, (0,), biasn_ref[...])], [values(vn_ref, (0,))])
        o16 = acc_ref[...] / l_ref[...]
        og = o16[:half] - lam_ref[0] * o16[half:]
        o_ref[0] = _rms(og) * sg_ref[...] * out_scale


def _attn_decode(layer, page_table, lam, q16, bias, bias_new, k_new, v_new, sg, ck, cv, *, out_scale):
    db, n_pages = page_table.shape
    n_pg = PAGES_PER_STEP
    page = ck.shape[-1]
    rows_v = cv.shape[2]
    pt_flat = page_table.reshape(-1)

    def kmap(i):
        return lambda b, j, pt: (layer, pt[b * n_pages + j * n_pg + i], 0, 0, 0, 0)

    def vmap(i):
        return lambda b, j, pt: (layer, pt[b * n_pages + j * n_pg + i], 0, 0)

    per_b = lambda b, j, pt: (b, 0, 0)
    const2 = lambda b, j, pt: (0, 0)
    kblk = (KV_HEADS, 2, HEAD_DIM, page)
    in_specs = [pl.BlockSpec(memory_space=pltpu.SMEM),
                pl.BlockSpec((1, 2 * N_HEADS, HEAD_DIM), per_b),
                pl.BlockSpec(bias.shape, const2),
                pl.BlockSpec(bias_new.shape, const2),
                pl.BlockSpec((1,) + kblk, lambda b, j, pt: (b, 0, 0, 0, 0)),
                pl.BlockSpec((1, rows_v, V_DIM), per_b),
                pl.BlockSpec((1, LANES), const2)]
    in_specs += [pl.BlockSpec((1, 1) + kblk, kmap(i)) for i in range(n_pg)]
    in_specs += [pl.BlockSpec((1, 1, rows_v, V_DIM), vmap(i)) for i in range(n_pg)]
    grid_spec = pltpu.PrefetchScalarGridSpec(
        num_scalar_prefetch=1,
        grid=(db, n_pages // n_pg),
        in_specs=in_specs,
        out_specs=pl.BlockSpec((1, N_HEADS, V_DIM), per_b),
        scratch_shapes=[pltpu.VMEM((2 * N_HEADS, 1), F32),
                        pltpu.VMEM((2 * N_HEADS, 1), F32),
                        pltpu.VMEM((2 * N_HEADS, V_DIM), F32)])
    return pl.pallas_call(
        functools.partial(_decode_kernel, n_pg=n_pg, out_scale=out_scale),
        grid_spec=grid_spec,
        out_shape=jax.ShapeDtypeStruct((db, N_HEADS, V_DIM), F32),
        compiler_params=pltpu.CompilerParams(dimension_semantics=("parallel", "arbitrary"),
                                             vmem_limit_bytes=VMEM_LIMIT),
        name="attn_decode",
    )(pt_flat, lam, q16, bias, bias_new, k_new, v_new, sg, *([ck] * n_pg), *([cv] * n_pg))


def _split_bf16(x):
    hi = x.astype(BF16)
    return hi, (x - hi.astype(F32)).astype(BF16)


def _merge_kernel(mixed_ref, o_ref, ga_ref, gb_ref, x_ref, g1_ref, sc_ref, sh_ref, wa_ref, wb_ref,
                  wo_ref, n2g_ref, rwh_ref, rwl_ref, rb_ref, tri_ref,
                  xo_ref, h2_ref, idx_ref, gate_ref, rank_ref, cnt_ref, run_ref):
    @pl.when((pl.program_id(0) == 0) & (pl.program_id(1) == 0))
    def _():
        run_ref[...] = jnp.zeros(run_ref.shape, F32)

    a = jnp.dot(mixed_ref[0], wa_ref[...], preferred_element_type=F32)
    b = jnp.dot(o_ref[0], wb_ref[...], preferred_element_type=F32)
    merged = ga_ref[0].astype(F32) * a + gb_ref[0].astype(F32) * b
    xo = x_ref[0] + g1_ref[0] * jnp.dot(merged.astype(BF16), wo_ref[...], preferred_element_type=F32)
    xo_ref[0] = xo
    h2 = _rms(xo) * n2g_ref[...] * (1.0 + sc_ref[0]) + sh_ref[0]
    hi, lo = _split_bf16(h2)
    tm = hi.shape[0]
    hi32 = hi.astype(F32)
    for j in range(SUB):
        h2_ref[0, pl.ds(j, tm, stride=SUB), :] = hi32[:, j * LANES:(j + 1) * LANES]
    lg = (jnp.dot(hi, rwh_ref[...], preferred_element_type=F32)
          + jnp.dot(lo, rwh_ref[...], preferred_element_type=F32)
          + jnp.dot(hi, rwl_ref[...], preferred_element_type=F32)
          + rb_ref[...])

    lane = lax.broadcasted_iota(jnp.int32, lg.shape, 1)
    work = lg
    sels, vals, idxs = [], [], []
    for _ in range(TOP_K):
        mx = jnp.max(work, axis=-1, keepdims=True)
        ik = jnp.min(jnp.where(work == mx, lane, LANES), axis=-1, keepdims=True)
        sel = lane == ik
        work = jnp.where(sel, -jnp.inf, work)
        sels.append(sel)
        vals.append(mx)
        idxs.append(ik)
    es = [jnp.exp(v - vals[0]) for v in vals]
    den = es[0]
    for e in es[1:]:
        den = den + e

    onehot = sels[0].astype(F32)
    for sel in sels[1:]:
        onehot = onehot + sel.astype(F32)
    base = jnp.dot(tri_ref[...], onehot.astype(BF16), preferred_element_type=F32) + run_ref[...]
    idx_out = jnp.zeros(lg.shape, jnp.int32)
    gate_out = jnp.zeros(lg.shape, F32)
    rank_out = jnp.zeros(lg.shape, jnp.int32)
    for k in range(TOP_K):
        rk = jnp.sum(jnp.where(sels[k], base, 0.0), axis=-1, keepdims=True)
        idx_out = jnp.where(lane == k, idxs[k], idx_out)
        gate_out = jnp.where(lane == k, es[k] / den, gate_out)
        rank_out = jnp.where(lane == k, rk.astype(jnp.int32), rank_out)
    idx_ref[0] = idx_out
    gate_ref[0] = gate_out
    rank_ref[0] = rank_out
    run_ref[...] = run_ref[...] + jnp.sum(onehot, axis=0, keepdims=True)
    cnt_ref[...] = run_ref[...]


def _merge(mixed, o, ga, gb, x, g1, sc, sh, wa, wb, wo, n2g, rwh, rwl, rb, *, tm):
    nb, s, d = x.shape
    r = g1.shape[1]
    row = lambda b, i: (b, i, 0)
    mod = (lambda b, i: (b, i, 0)) if r == s else (lambda b, i: (b, 0, 0))
    rblk = tm if r == s else 1
    tok = pl.BlockSpec((1, tm, d), row)
    mods = pl.BlockSpec((1, rblk, d), mod)
    route = pl.BlockSpec((1, tm, LANES), row)
    tri = jnp.tril(jnp.ones((tm, tm), F32), -1).astype(BF16)
    return pl.pallas_call(
        _merge_kernel,
        grid=(nb, s // tm),
        in_specs=[tok, tok, tok, tok, tok, mods, mods, mods,
                  _const_spec((d, d)), _const_spec((d, d)), _const_spec((d, d)),
                  _const_spec((1, d)), _const_spec((d, LANES)), _const_spec((d, LANES)),
                  _const_spec((1, LANES)), _const_spec((tm, tm))],
        out_specs=[tok, pl.BlockSpec((1, tm * SUB, LANES), row), route, route, route,
                   pl.BlockSpec((1, LANES), lambda b, i: (0, 0))],
        out_shape=[jax.ShapeDtypeStruct((nb, s, d), F32),
                   jax.ShapeDtypeStruct((nb, s * SUB, LANES), F32),
                   jax.ShapeDtypeStruct((nb, s, LANES), jnp.int32),
                   jax.ShapeDtypeStruct((nb, s, LANES), F32),
                   jax.ShapeDtypeStruct((nb, s, LANES), jnp.int32),
                   jax.ShapeDtypeStruct((1, LANES), F32)],
        scratch_shapes=[pltpu.VMEM((1, LANES), F32)],
        compiler_params=pltpu.CompilerParams(dimension_semantics=("arbitrary", "arbitrary"),
                                             vmem_limit_bytes=VMEM_LIMIT),
        name="merge",
    )(mixed, o, ga, gb, x, g1, sc, sh, wa, wb, wo, n2g, rwh, rwl, rb, tri)


def _wprep_kernel(w1_ref, w2_ref, perm_ref, g_ref, l_ref, w2o_ref):
    for j in range(w1_ref.shape[-1] // MXU_DIM):
        z = jnp.dot(w1_ref[0, 0, :, j * MXU_DIM:(j + 1) * MXU_DIM].astype(BF16), perm_ref[...],
                    preferred_element_type=F32)
        g_ref[0, :, j * LANES:(j + 1) * LANES] = z[:, :LANES].astype(BF16)
        l_ref[0, :, j * LANES:(j + 1) * LANES] = z[:, LANES:].astype(BF16)
    w2o_ref[0] = w2_ref[0, 0].astype(BF16)


def _wprep(layer, w1, w2):
    _, n_e, d, f2 = w1.shape
    f = f2 // 2
    src = jnp.concatenate([jnp.arange(0, MXU_DIM, 2), jnp.arange(1, MXU_DIM, 2)])
    perm = (jnp.arange(MXU_DIM)[:, None] == src[None, :]).astype(BF16)
    return pl.pallas_call(
        _wprep_kernel,
        grid=(n_e,),
        in_specs=[pl.BlockSpec((1, 1, d, f2), lambda e: (layer, e, 0, 0)),
                  pl.BlockSpec((1, 1, f, d), lambda e: (layer, e, 0, 0)),
                  _const_spec((MXU_DIM, MXU_DIM))],
        out_specs=[pl.BlockSpec((1, d, f), lambda e: (e, 0, 0)),
                   pl.BlockSpec((1, d, f), lambda e: (e, 0, 0)),
                   pl.BlockSpec((1, f, d), lambda e: (e, 0, 0))],
        out_shape=[jax.ShapeDtypeStruct((n_e, d, f), BF16),
                   jax.ShapeDtypeStruct((n_e, d, f), BF16),
                   jax.ShapeDtypeStruct((n_e, f, d), BF16)],
        compiler_params=pltpu.CompilerParams(dimension_semantics=("parallel",),
                                             vmem_limit_bytes=VMEM_LIMIT),
        name="wprep",
    )(w1, w2, perm)


def _expert_kernel(be_ref, nu_ref, x_ref, w1g_ref, w1l_ref, b1g_ref, b1l_ref, w2_ref, b2_ref, y_ref):
    del be_ref
    tb = x_ref.shape[0] // SUB

    @pl.when(pl.program_id(0) < nu_ref[0])
    def _():
        x = jnp.concatenate([x_ref[pl.ds(j, tb, stride=SUB), :] for j in range(SUB)], axis=1).astype(BF16)
        glu = jnp.minimum(jnp.dot(x, w1g_ref[0], preferred_element_type=F32) + b1g_ref[0], SWIGLU_LIMIT)
        lin = jnp.clip(jnp.dot(x, w1l_ref[0], preferred_element_type=F32) + b1l_ref[0],
                       -SWIGLU_LIMIT, SWIGLU_LIMIT)
        act = glu * jax.nn.sigmoid(SWIGLU_ALPHA * glu) * (lin + 1.0)
        y = jnp.dot(act.astype(BF16), w2_ref[0], preferred_element_type=F32) + b2_ref[0]
        for j in range(SUB):
            y_ref[pl.ds(j, tb, stride=SUB), :] = y[:, j * LANES:(j + 1) * LANES]

    @pl.when(pl.program_id(0) >= nu_ref[0])
    def _():
        y_ref[...] = jnp.zeros(y_ref.shape, F32)


def _experts(block_expert, n_used, xs, w1g, w1l, b1g, b1l, w2, b2, *, tb):
    n_slots = xs.shape[0] // SUB
    n_e, d, f = w1g.shape
    emap = lambda i, be, nu: (be[i], 0, 0)
    grid_spec = pltpu.PrefetchScalarGridSpec(
        num_scalar_prefetch=2,
        grid=(n_slots // tb,),
        in_specs=[pl.BlockSpec((tb * SUB, LANES), lambda i, be, nu: (jnp.minimum(i, nu[0] - 1), 0)),
                  pl.BlockSpec((1, d, f), emap),
                  pl.BlockSpec((1, d, f), emap),
                  pl.BlockSpec((1, 1, f), emap),
                  pl.BlockSpec((1, 1, f), emap),
                  pl.BlockSpec((1, f, d), emap),
                  pl.BlockSpec((1, 1, d), emap)],
        out_specs=pl.BlockSpec((tb * SUB, LANES), lambda i, be, nu: (i, 0)))
    return pl.pallas_call(
        _expert_kernel,
        grid_spec=grid_spec,
        out_shape=jax.ShapeDtypeStruct((n_slots * SUB, LANES), F32),
        compiler_params=pltpu.CompilerParams(dimension_semantics=("arbitrary",),
                                             vmem_limit_bytes=VMEM_LIMIT),
        name="experts",
    )(block_expert, n_used, xs, w1g, w1l, b1g.reshape(n_e, 1, f), b1l.reshape(n_e, 1, f), w2,
      b2.reshape(n_e, 1, d))


def _dispatch_kernel(dest_ref, h_ref, xs_init_ref, xs_ref, sem):
    del xs_init_ref
    tm = h_ref.shape[0] // SUB

    def issue(t, carry):
        for k in range(TOP_K):
            slot = dest_ref[0, 0, t * TOP_K + k]
            pltpu.make_async_copy(h_ref.at[pl.ds(pl.multiple_of(t * SUB, SUB), SUB)],
                                  xs_ref.at[pl.ds(pl.multiple_of(slot * SUB, SUB), SUB)], sem).start()
        return carry

    lax.fori_loop(0, tm, issue, 0)
    for _ in range(TOP_K):
        pltpu.make_async_copy(h_ref, xs_ref.at[pl.ds(0, tm * SUB)], sem).wait()


def _dispatch(dest_tiles, h, n_slots, xs_init):
    if xs_init is None:
        xs_init = jnp.zeros((n_slots * SUB, LANES), F32)
    n_tiles, _, n = dest_tiles.shape
    tm = n // TOP_K
    return pl.pallas_call(
        _dispatch_kernel,
        grid=(n_tiles,),
        in_specs=[pl.BlockSpec((1, 1, n), lambda i: (i, 0, 0), memory_space=pltpu.SMEM),
                  pl.BlockSpec((tm * SUB, LANES), lambda i: (i, 0)),
                  pl.BlockSpec(memory_space=pl.ANY)],
        out_specs=pl.BlockSpec(memory_space=pl.ANY),
        out_shape=jax.ShapeDtypeStruct((n_slots * SUB, LANES), F32),
        scratch_shapes=[pltpu.SemaphoreType.DMA(())],
        input_output_aliases={2: 0},
        compiler_params=pltpu.CompilerParams(dimension_semantics=("arbitrary",)),
        name="dispatch",
    )(dest_tiles, h, xs_init)


def _combine_kernel(dcur_ref, dnext_ref, gate_ref, x_ref, g2_ref, yb_ref, o_ref, buf_ref, sem):
    i = pl.program_id(0)
    tm = x_ref.shape[0]

    def gather(d_ref, slot):
        def issue(t, carry):
            for k in range(TOP_K):
                src = pl.multiple_of(d_ref[0, 0, t * TOP_K + k] * SUB, SUB)
                pltpu.make_async_copy(yb_ref.at[pl.ds(src, SUB)],
                                      buf_ref.at[slot, k, pl.ds(pl.multiple_of(t * SUB, SUB), SUB)],
                                      sem.at[slot]).start()
            return carry
        lax.fori_loop(0, tm, issue, 0)

    @pl.when(i == 0)
    def _():
        gather(dcur_ref, 0)

    slot = i % 2

    @pl.when(i + 1 < pl.num_programs(0))
    def _():
        gather(dnext_ref, 1 - slot)

    for k in range(TOP_K):
        pltpu.make_async_copy(yb_ref.at[pl.ds(0, tm * SUB)], buf_ref.at[slot, k], sem.at[slot]).wait()
    gates = gate_ref[...]
    for j in range(SUB):
        cs = slice(j * LANES, (j + 1) * LANES)
        y = buf_ref[slot, 0, pl.ds(j, tm, stride=SUB), :] * gates[:, 0:1]
        for k in range(1, TOP_K):
            y = y + buf_ref[slot, k, pl.ds(j, tm, stride=SUB), :] * gates[:, k:k + 1]
        o_ref[:, cs] = x_ref[:, cs] + g2_ref[0, :, cs] * y


def _combine(dest_tiles, gates, x, g2, yb, *, tiles_per_seq):
    t, d = x.shape
    n_tiles, _, n = dest_tiles.shape
    tm = n // TOP_K
    if g2.shape[1] == 1:
        g2_spec = pl.BlockSpec((1, 1, d), lambda i: (i // tiles_per_seq, 0, 0))
    else:
        g2_spec = pl.BlockSpec((1, tm, d), lambda i: (i // tiles_per_seq, i % tiles_per_seq, 0))
    dspec = lambda f: pl.BlockSpec((1, 1, n), f, memory_space=pltpu.SMEM)
    return pl.pallas_call(
        _combine_kernel,
        grid=(n_tiles,),
        in_specs=[dspec(lambda i: (i, 0, 0)),
                  dspec(lambda i: (jnp.minimum(i + 1, n_tiles - 1), 0, 0)),
                  pl.BlockSpec((tm, LANES), lambda i: (i, 0)),
                  pl.BlockSpec((tm, d), lambda i: (i, 0)),
                  g2_spec,
                  pl.BlockSpec(memory_space=pl.ANY)],
        out_specs=pl.BlockSpec((tm, d), lambda i: (i, 0)),
        out_shape=jax.ShapeDtypeStruct((t, d), F32),
        scratch_shapes=[pltpu.VMEM((2, TOP_K, tm * SUB, LANES), F32),
                        pltpu.SemaphoreType.DMA((2,))],
        compiler_params=pltpu.CompilerParams(dimension_semantics=("arbitrary",),
                                             vmem_limit_bytes=VMEM_LIMIT),
        name="combine",
    )(dest_tiles, dest_tiles, gates, x, g2, yb)


def _moe(h2, idx, gates, rank, cnt, x, g2, w1g, w1l, b1g, b1l, w2, b2, *, tb, tm, xs_init=None):
    nb, s, d = x.shape
    t = nb * s
    n_assign = t * TOP_K
    counts = cnt[0, :N_EXPERTS].astype(jnp.int32)
    padded = (counts + tb - 1) // tb * tb
    pad_ends = jnp.cumsum(padded)
    pad_starts = pad_ends - padded
    idx4 = idx.reshape(t, LANES)[:, :TOP_K]
    rank4 = rank.reshape(t, LANES)[:, :TOP_K]
    experts = jnp.arange(N_EXPERTS, dtype=jnp.int32)
    dest = rank4 + jnp.sum(jnp.where(idx4[:, :, None] == experts, pad_starts, 0), axis=-1)
    dest_tiles = dest.reshape(t // tm, 1, tm * TOP_K)
    n_blocks = -(-n_assign // tb) + N_EXPERTS
    block_start = jnp.arange(n_blocks, dtype=jnp.int32) * tb
    block_expert = jnp.minimum(jnp.sum((block_start[:, None] >= pad_ends[None, :]).astype(jnp.int32), axis=1),
                               N_EXPERTS - 1)
    n_used = (pad_ends[-1:] // tb).astype(jnp.int32)
    xs = _dispatch(dest_tiles, h2.reshape(t * SUB, LANES), n_blocks * tb, xs_init)
    yb = _experts(block_expert, n_used, xs, w1g, w1l, b1g, b1l, w2, b2, tb=tb)
    out = _combine(dest_tiles, gates.reshape(t, LANES), x.reshape(t, d), g2, yb, tiles_per_seq=s // tm)
    return out.reshape(nb, s, d), yb


def _lambda_init(layer):
    return 0.8 - 0.6 * math.exp(-0.3 * layer)


def _decode_bias(n_pos, visible):
    col = jnp.arange(n_pos * KV_HEADS, dtype=jnp.int32)[None, :]
    row = jnp.arange(2 * N_HEADS, dtype=jnp.int32)[:, None]
    ok = ((col // n_pos) == ((row % N_HEADS) // GQA_GROUP)) & ((col % n_pos) < visible)
    return jnp.where(ok, 0.0, NEG).astype(F32)


def kernel(x_prompt, x_sample, c_prompt, c_sample, cache_k, cache_v, page_table, ada_w, ada_b, norm1_g, norm2_g, w_in, q_norm_g, k_norm_g, lambda_q1, lambda_k1, lambda_q2, lambda_k2, subln_g, ln_v_g, ln_v_b, w_spatial, b_spatial, w_branch_a, w_branch_b, w_out, router_w, router_b, w1, b1, w2, b2):
    depth = ada_w.shape[0]
    nb, seq, d = x_prompt.shape
    db, ds, _ = x_sample.shape
    assert ds == 1, "sampled tokens open a fresh chunk one row at a time"
    assert d == SUB * LANES, "MoE rows are moved as one float32 tile per token"
    n_phys, page = cache_k.shape[1], cache_k.shape[2]

    mods = _ada_mods(jnp.concatenate([c_prompt, c_sample], axis=0), ada_w, ada_b)
    ck = jnp.transpose(cache_k, (0, 1, 3, 4, 5, 2))
    cv = cache_v.reshape(depth, n_phys, page * KV_HEADS, V_DIM)
    bias = _decode_bias(page, page)
    bias_new = _decode_bias(page, ds)
    tri = jnp.tril(jnp.ones((CHUNK, CHUNK), F32))

    yp = x_prompt
    ys = x_sample.reshape(1, db, d)
    kp_l, vp_l, ks_l, vs_l, cv_l = [], [], [], [], []
    slots_p = slots_s = None
    for l in range(depth):
        lam_init = _lambda_init(l)
        lam = (jnp.exp(jnp.sum(lambda_q1[l] * lambda_k1[l])) - jnp.exp(jnp.sum(lambda_q2[l] * lambda_k2[l]))
               + lam_init).reshape(1).astype(F32)
        m6 = mods[l].reshape(nb + db, 6, d)
        mp = [m6[:nb, i][:, None, :] for i in range(6)]
        msm = [m6[nb:, i][None, :, :] for i in range(6)]
        w_in_b = w_in[l].astype(BF16)
        n1g = norm1_g[l][None, :]
        n2g = norm2_g[l][None, :]
        lng, lnb = ln_v_g[l][None, :], ln_v_b[l][None, :]
        qg = jnp.tile(q_norm_g[l], 2)[None, :]
        kg = jnp.tile(k_norm_g[l], 2)[None, :]
        sg = subln_g[l][None, :]
        sg_col = subln_g[l][:, None]
        ws_p = (w_spatial[l] * tri).astype(BF16)
        bs_p = jnp.repeat(b_spatial[l].T, LANES, axis=1)
        ws_s = jnp.repeat(w_spatial[l][:, 0, 0], LANES)[None, :]
        bs_s = jnp.repeat(b_spatial[l][:, 0], LANES)[None, :]
        wa, wb, wo = (w_branch_a[l].astype(BF16), w_branch_b[l].astype(BF16), w_out[l].astype(BF16))
        rw = jnp.pad(router_w[l], ((0, 0), (0, LANES - N_EXPERTS)))
        rwh, rwl = _split_bf16(rw)
        rb = jnp.pad(router_b[l], (0, LANES - N_EXPERTS), constant_values=NEG)[None, :]
        w1g, w1l, w2p = _wprep(l, w1, w2)
        b1g, b1l = b1[l][:, 0::2], b1[l][:, 1::2]
        ew = (w1g, w1l, b1g, b1l, w2p, b2[l])
        out_scale = 1.0 - lam_init

        mixed, q, k, v, kb, vb, ga, gb = _inproj(yp, mp[1], mp[0], n1g, w_in_b, lng, lnb, qg, kg,
                                                 ws_p, bs_p, sample=False, tm=TM_DENSE,
                                                 q_scale=HEAD_DIM ** -0.5 * LOG2E)
        o = _attn_prompt(lam, q, kb, vb, sg_col, out_scale=out_scale, tq=TQ, tk=TK)
        xo, h2, *route = _merge(mixed, o, ga, gb, yp, mp[2], mp[4], mp[3], wa, wb, wo, n2g, rwh, rwl, rb,
                                tm=TM_DENSE)
        yp, slots_p = _moe(h2, *route, xo, mp[5], *ew, tb=TB_PROMPT, tm=TM_DENSE, xs_init=slots_p)
        kp_l.append(k.reshape(nb, KV_HEADS, 2, HEAD_DIM, seq).transpose(0, 4, 1, 2, 3))
        vp_l.append(v.reshape(nb, seq, KV_HEADS, V_DIM))

        mixed, q, k, v, kb, vb, ga, gb, cvs = _inproj(ys, msm[1], msm[0], n1g, w_in_b, lng, lnb, qg, kg,
                                                      ws_s, bs_s, sample=True, tm=db,
                                                      q_scale=HEAD_DIM ** -0.5)
        q16 = q.reshape(db, N_HEADS, 2, HEAD_DIM).transpose(0, 2, 1, 3).reshape(db, 2 * N_HEADS, HEAD_DIM)
        k_new = jnp.pad(k.reshape(db, KV_HEADS, 2, HEAD_DIM, 1), ((0, 0),) * 4 + ((0, page - 1),))
        v_new = jnp.pad(v.reshape(db, KV_HEADS, V_DIM), ((0, 0), (0, (page - 1) * KV_HEADS), (0, 0)))
        o = _attn_decode(l, page_table, lam, q16, bias, bias_new, k_new, v_new, sg, ck, cv,
                         out_scale=out_scale)
        o = o.reshape(1, db, d).astype(BF16)
        xo, h2, *route = _merge(mixed, o, ga, gb, ys, msm[2], msm[4], msm[3], wa, wb, wo, n2g, rwh, rwl, rb,
                                tm=db)
        ys, slots_s = _moe(h2, *route, xo, msm[5], *ew, tb=TB_SAMPLE, tm=db, xs_init=slots_s)
        ks_l.append(k.reshape(db, ds, KV_HEADS, 2, HEAD_DIM))
        vs_l.append(v.reshape(db, ds, KV_HEADS, V_DIM))
        cv_l.append(cvs.reshape(db, ds, d))

    return (yp, ys.reshape(db, ds, d), jnp.stack(kp_l), jnp.stack(vp_l), jnp.stack(ks_l),
            jnp.stack(vs_l), jnp.stack(cv_l))
```

```python
import functools
import math

import jax
import jax.numpy as jnp
from jax import lax
from jax.experimental import pallas as pl
from jax.experimental.pallas import tpu as pltpu

F32 = jnp.float32
BF16 = jnp.bfloat16

LANES = 128
MXU_DIM = 256
HEAD_DIM = 64
N_HEADS = 8
KV_HEADS = 4
GQA_GROUP = N_HEADS // KV_HEADS
V_DIM = 2 * HEAD_DIM
CHUNK = 128
A_GROUPS = 8
N_EXPERTS = 32
TOP_K = 4
SUB = 8
SWIGLU_ALPHA = 1.702
SWIGLU_LIMIT = 7.0
LOG2E = math.log2(math.e)
NEG = -1e30

TM_DENSE = 512
TQ = 512
TK = 512
ONES_ROWS = 16
PAGES_PER_STEP = 16
TB_PROMPT = 256
TB_SAMPLE = 16
VMEM_LIMIT = 48 * 1024 * 1024


def _const_spec(shape):
    nd = len(shape)
    return pl.BlockSpec(shape, lambda *_: (0,) * nd, pipeline_mode=pl.Buffered(1))


def _gelu(x):
    return x * (lax.erf(x * (1.0 / math.sqrt(2.0))) + 1.0) * 0.5


def _rms(x, eps=1e-6):
    return x * lax.rsqrt(jnp.mean(x * x, axis=-1, keepdims=True) + eps)


def _ada_kernel(c_ref, w_ref, b_ref, o_ref):
    c = c_ref[...]
    sc = (c * jax.nn.sigmoid(c)).astype(BF16)
    o_ref[0] = jnp.dot(sc, w_ref[0].astype(BF16), preferred_element_type=F32) + b_ref[0]


def _ada_mods(c_all, ada_w, ada_b):
    depth, d, n = ada_w.shape
    r = c_all.shape[0]
    tn = 512
    return pl.pallas_call(
        _ada_kernel,
        grid=(depth, n // tn),
        in_specs=[pl.BlockSpec((r, d), lambda l, j: (0, 0)),
                  pl.BlockSpec((1, d, tn), lambda l, j: (l, 0, j)),
                  pl.BlockSpec((1, 1, tn), lambda l, j: (l, 0, j))],
        out_specs=pl.BlockSpec((1, r, tn), lambda l, j: (l, 0, j)),
        out_shape=jax.ShapeDtypeStruct((depth, r, n), F32),
        compiler_params=pltpu.CompilerParams(dimension_semantics=("parallel", "parallel")),
        name="ada",
    )(c_all, ada_w, ada_b.reshape(depth, 1, n))


def _half_rms(t, g_row, first):
    s = t * t
    s1 = jnp.sum(jnp.where(first, s, 0.0), axis=-1, keepdims=True)
    s2 = jnp.sum(jnp.where(first, 0.0, s), axis=-1, keepdims=True)
    r = jnp.where(first, lax.rsqrt(s1 * (1.0 / HEAD_DIM) + 1e-6),
                  lax.rsqrt(s2 * (1.0 / HEAD_DIM) + 1e-6))
    return t * r * g_row


def _inproj_kernel(x_ref, sc_ref, sh_ref, n1g_ref, w_ref, lng_ref, lnb_ref, qg_ref, kg_ref,
                   ws_ref, bs_ref, mixed_ref, q_ref, k_ref, v_ref, kb_ref, vb_ref, ga_ref, gb_ref,
                   *cv_ref, sample, q_scale):
    d = x_ref.shape[-1]
    tm = x_ref.shape[1]
    x = x_ref[0]
    h = (_rms(x) * n1g_ref[...] * (1.0 + sc_ref[0]) + sh_ref[0]).astype(BF16)

    def proj(lo, width):
        return jnp.dot(h, w_ref[:, lo:lo + width], preferred_element_type=F32)

    u = _gelu(proj(0, d))
    va = _gelu(proj(d, d))
    dv = va - jnp.mean(va, axis=-1, keepdims=True)
    van = dv * lax.rsqrt(jnp.mean(dv * dv, axis=-1, keepdims=True) + 1e-5) * lng_ref[...] + lnb_ref[...]
    if sample:
        cv_ref[0][0] = van
        mixed_ref[0] = (u * (van * ws_ref[...] + bs_ref[...])).astype(BF16)
    else:
        vb16 = van.astype(BF16)
        for c in range(tm // CHUNK):
            rs = slice(c * CHUNK, (c + 1) * CHUNK)
            for g in range(A_GROUPS):
                cs = slice(g * LANES, (g + 1) * LANES)
                mix = jnp.dot(ws_ref[g], vb16[rs, cs], preferred_element_type=F32) + bs_ref[:, cs]
                mixed_ref[0, rs, cs] = (u[rs, cs] * mix).astype(BF16)

    first = lax.broadcasted_iota(jnp.int32, (1, LANES), 1) < HEAD_DIM
    zq = proj(2 * d, N_HEADS * LANES)
    for hh in range(N_HEADS):
        cs = slice(hh * LANES, (hh + 1) * LANES)
        q_ref[0, :, cs] = (_half_rms(zq[:, cs], qg_ref[...], first) * q_scale).astype(BF16)
    zk = proj(2 * d + N_HEADS * LANES, KV_HEADS * LANES)
    for hh in range(KV_HEADS):
        cs = slice(hh * LANES, (hh + 1) * LANES)
        kn = _half_rms(zk[:, cs], kg_ref[...], first)
        if sample:
            k_ref[0, :, cs] = kn
        else:
            k_ref[0, cs, :] = kn.T
        kb_ref[0, :, cs] = kn.astype(BF16)
    zv = proj(2 * d + (N_HEADS + KV_HEADS) * LANES, KV_HEADS * LANES)
    if sample:
        v_ref[0] = zv
    else:
        for hh in range(KV_HEADS):
            v_ref[0, pl.ds(hh, tm, stride=KV_HEADS), :] = zv[:, hh * LANES:(hh + 1) * LANES]
    vb_ref[0] = zv.astype(BF16)
    base = 2 * d + (N_HEADS + 2 * KV_HEADS) * LANES
    ga_ref[0] = jax.nn.sigmoid(proj(base, d)).astype(BF16)
    gb_ref[0] = jax.nn.sigmoid(proj(base + d, d)).astype(BF16)


def _inproj(x, sc, sh, n1g, w_in, lng, lnb, qg, kg, ws, bs, *, sample, tm, q_scale):
    nb, s, d = x.shape
    r = sc.shape[1]
    kw = KV_HEADS * LANES
    row = lambda b, i: (b, i, 0)
    mod = (lambda b, i: (b, i, 0)) if r == s else (lambda b, i: (b, 0, 0))
    rblk = tm if r == s else 1
    in_specs = [pl.BlockSpec((1, tm, d), row),
                pl.BlockSpec((1, rblk, d), mod),
                pl.BlockSpec((1, rblk, d), mod),
                _const_spec((1, d)),
                _const_spec(w_in.shape),
                _const_spec((1, d)), _const_spec((1, d)),
                _const_spec((1, LANES)), _const_spec((1, LANES)),
                _const_spec(ws.shape), _const_spec(bs.shape)]
    out_shapes = [jax.ShapeDtypeStruct((nb, s, d), BF16),
                  jax.ShapeDtypeStruct((nb, s, d), BF16),
                  jax.ShapeDtypeStruct((nb, s, kw), F32),
                  jax.ShapeDtypeStruct((nb, s, kw), F32),
                  jax.ShapeDtypeStruct((nb, s, kw), BF16),
                  jax.ShapeDtypeStruct((nb, s, kw), BF16),
                  jax.ShapeDtypeStruct((nb, s, d), BF16),
                  jax.ShapeDtypeStruct((nb, s, d), BF16)]
    if sample:
        out_shapes.append(jax.ShapeDtypeStruct((nb, s, d), F32))
    out_specs = [pl.BlockSpec((1, tm, o.shape[-1]), row) for o in out_shapes]
    if not sample:
        out_shapes[2] = jax.ShapeDtypeStruct((nb, kw, s), F32)
        out_shapes[3] = jax.ShapeDtypeStruct((nb, s * KV_HEADS, V_DIM), F32)
        out_specs[2] = pl.BlockSpec((1, kw, tm), lambda b, i: (b, 0, i))
        out_specs[3] = pl.BlockSpec((1, tm * KV_HEADS, V_DIM), row)
    return pl.pallas_call(
        functools.partial(_inproj_kernel, sample=sample, q_scale=q_scale),
        grid=(nb, s // tm),
        in_specs=in_specs,
        out_specs=out_specs,
        out_shape=out_shapes,
        compiler_params=pltpu.CompilerParams(dimension_semantics=("parallel", "parallel"),
                                             vmem_limit_bytes=VMEM_LIMIT),
        name="inproj_sample" if sample else "inproj",
    )(x, sc, sh, n1g, w_in, lng, lnb, qg, kg, ws, bs)


def _attn_kernel(lam_ref, q_ref, k_ref, v_ref, sg_ref, o_ref, qs_ref, vt_ref, sa_ref, sb_ref, m_ref, acc_ref,
                 *, tq, tk, out_scale):
    qi = pl.program_id(2)
    n_kt = vt_ref.shape[0]

    @pl.when(qi == 0)
    def _():
        def fill(j, carry):
            start = pl.multiple_of(j * tk, tk)
            vt_ref[j, :LANES, :] = v_ref[0, pl.ds(start, tk), :].astype(F32).T.astype(BF16)
            vt_ref[j, LANES:, :] = jnp.ones((ONES_ROWS, tk), BF16)
            return carry
        lax.fori_loop(0, n_kt, fill, 0)

    first = lax.broadcasted_iota(jnp.int32, (LANES, 1), 0) < HEAD_DIM
    for g in range(GQA_GROUP):
        qt = q_ref[0, :, g * LANES:(g + 1) * LANES].astype(F32).T
        qs_ref[:, (2 * g) * tq:(2 * g + 1) * tq] = jnp.where(first, qt, 0.0).astype(BF16)
        qs_ref[:, (2 * g + 1) * tq:(2 * g + 2) * tq] = jnp.where(first, 0.0, qt).astype(BF16)
    m_ref[...] = jnp.full(m_ref.shape, NEG, F32)
    acc_ref[...] = jnp.zeros(acc_ref.shape, F32)

    def qk_into(s_ref, kj):
        start = pl.multiple_of(kj * tk, tk)
        s_ref[...] = jnp.dot(k_ref[0, pl.ds(start, tk), :], qs_ref[...], preferred_element_type=F32)

    def process(s_ref, kj, masked):
        s = s_ref[...]
        if masked:
            kpos = kj * tk + lax.broadcasted_iota(jnp.int32, s.shape, 0)
            qpos = qi * tq + (lax.broadcasted_iota(jnp.int32, s.shape, 1) & (tq - 1))
            s = jnp.where(kpos <= qpos, s, NEG)
        m_prev = m_ref[...]
        m_new = jnp.maximum(m_prev, jnp.max(s, axis=0, keepdims=True))
        alpha = jnp.exp2(m_prev - m_new)
        p = jnp.exp2(s - m_new)
        acc_ref[...] = alpha * acc_ref[...] + jnp.dot(vt_ref[kj], p.astype(BF16),
                                                      preferred_element_type=F32)
        m_ref[...] = m_new

    n_full = (qi * tq) // tk
    qk_into(sa_ref, 0)

    def pair(i, carry):
        qk_into(sb_ref, 2 * i + 1)
        process(sa_ref, 2 * i, False)
        qk_into(sa_ref, 2 * i + 2)
        process(sb_ref, 2 * i + 1, False)
        return carry

    lax.fori_loop(0, n_full // 2, pair, 0)

    @pl.when(n_full % 2 == 1)
    def _():
        qk_into(sb_ref, n_full)
        process(sa_ref, n_full - 1, False)
        process(sb_ref, n_full, True)

    @pl.when(n_full % 2 == 0)
    def _():
        process(sa_ref, n_full, True)

    o_all = acc_ref[:LANES, :] / acc_ref[LANES:LANES + 1, :]
    lam = lam_ref[0]
    for g in range(GQA_GROUP):
        og = o_all[:, (2 * g) * tq:(2 * g + 1) * tq] - lam * o_all[:, (2 * g + 1) * tq:(2 * g + 2) * tq]
        og = og * lax.rsqrt(jnp.mean(og * og, axis=0, keepdims=True) + 1e-6) * sg_ref[...] * out_scale
        o_ref[0, :, g * LANES:(g + 1) * LANES] = og.T.astype(BF16)


def _attn_prompt(lam, q, kb, vb, sg_col, *, out_scale, tq, tk):
    nb, s, d = q.shape
    gw = GQA_GROUP * LANES
    m = 2 * GQA_GROUP * tq
    return pl.pallas_call(
        functools.partial(_attn_kernel, tq=tq, tk=tk, out_scale=out_scale),
        grid=(nb, KV_HEADS, s // tq),
        in_specs=[pl.BlockSpec(memory_space=pltpu.SMEM),
                  pl.BlockSpec((1, tq, gw), lambda b, h, i: (b, i, h)),
                  pl.BlockSpec((1, s, LANES), lambda b, h, i: (b, 0, h)),
                  pl.BlockSpec((1, s, LANES), lambda b, h, i: (b, 0, h)),
                  _const_spec((LANES, 1))],
        out_specs=pl.BlockSpec((1, tq, gw), lambda b, h, i: (b, i, h)),
        out_shape=jax.ShapeDtypeStruct((nb, s, d), BF16),
        scratch_shapes=[pltpu.VMEM((LANES, m), BF16),
                        pltpu.VMEM((s // tk, LANES + ONES_ROWS, tk), BF16),
                        pltpu.VMEM((tk, m), F32),
                        pltpu.VMEM((tk, m), F32),
                        pltpu.VMEM((1, m), F32),
                        pltpu.VMEM((LANES + ONES_ROWS, m), F32)],
        compiler_params=pltpu.CompilerParams(
            dimension_semantics=("parallel", "parallel", "arbitrary"),
            vmem_limit_bytes=VMEM_LIMIT),
        name="attn_prompt",
    )(lam, q, kb, vb, sg_col)


def _decode_kernel(pt_ref, lam_ref, q_ref, bias_ref, biasn_ref, kn_ref, vn_ref, sg_ref, *rest,
                   n_pg, out_scale):
    k_refs = rest[:n_pg]
    v_refs = rest[n_pg:2 * n_pg]
    o_ref = rest[2 * n_pg]
    m_ref, l_ref, acc_ref = rest[2 * n_pg + 1:]
    del pt_ref
    j = pl.program_id(1)
    half = N_HEADS

    @pl.when(j == 0)
    def _():
        m_ref[...] = jnp.full(m_ref.shape, NEG, F32)
        l_ref[...] = jnp.zeros(l_ref.shape, F32)
        acc_ref[...] = jnp.zeros(acc_ref.shape, F32)

    q16 = q_ref[0]

    def scores(kref, idx, bias):
        parts = []
        for mp in range(2):
            kcat = jnp.concatenate([kref[idx + (kvh, mp)] for kvh in range(KV_HEADS)], axis=1)
            sm = jnp.dot(q16, kcat.astype(BF16), preferred_element_type=F32)
            parts.append(sm[mp * half:(mp + 1) * half])
        return jnp.concatenate(parts, axis=0) + bias

    def values(vref, idx):
        n = vref.shape[-2] // KV_HEADS
        return jnp.concatenate([vref[idx + (pl.ds(kvh, n, stride=KV_HEADS), slice(None))]
                                for kvh in range(KV_HEADS)], axis=0).astype(BF16)

    def update(s_list, v_list):
        s = jnp.concatenate(s_list, axis=1) if len(s_list) > 1 else s_list[0]
        m_prev = m_ref[...]
        m_new = jnp.maximum(m_prev, jnp.max(s, axis=-1, keepdims=True))
        alpha = jnp.exp(m_prev - m_new)
        p = jnp.exp(s - m_new)
        l_ref[...] = alpha * l_ref[...] + jnp.sum(p, axis=-1, keepdims=True)
        w = s_list[0].shape[1]
        pv = None
        for i, vv in enumerate(v_list):
            t = jnp.dot(p[:, i * w:(i + 1) * w].astype(BF16), vv, preferred_element_type=F32)
            pv = t if pv is None else pv + t
        acc_ref[...] = alpha * acc_ref[...] + pv
        m_ref[...] = m_new

    bias = bias_ref[...]
    update([scores(kr, (0, 0), bias) for kr in k_refs], [values(vr, (0, 0)) for vr in v_refs])

    @pl.when(j == pl.num_programs(1) - 1)
    def _():
        update([scores(kn_ref, (0,), biasn_ref[...])], [values(vn_ref, (0,))])
        o16 = acc_ref[...] / l_ref[...]
        og = o16[:half] - lam_ref[0] * o16[half:]
        o_ref[0] = _rms(og) * sg_ref[...] * out_scale


def _attn_decode(layer, page_table, lam, q16, bias, bias_new, k_new, v_new, sg, ck, cv, *, out_scale):
    db, n_pages = page_table.shape
    n_pg = PAGES_PER_STEP
    page = ck.shape[-1]
    rows_v = cv.shape[2]
    pt_flat = page_table.reshape(-1)

    def kmap(i):
        return lambda b, j, pt: (layer, pt[b * n_pages + j * n_pg + i], 0, 0, 0, 0)

    def vmap(i):
        return lambda b, j, pt: (layer, pt[b * n_pages + j * n_pg + i], 0, 0)

    per_b = lambda b, j, pt: (b, 0, 0)
    const2 = lambda b, j, pt: (0, 0)
    kblk = (KV_HEADS, 2, HEAD_DIM, page)
    in_specs = [pl.BlockSpec(memory_space=pltpu.SMEM),
                pl.BlockSpec((1, 2 * N_HEADS, HEAD_DIM), per_b),
                pl.BlockSpec(bias.shape, const2),
                pl.BlockSpec(bias_new.shape, const2),
                pl.BlockSpec((1,) + kblk, lambda b, j, pt: (b, 0, 0, 0, 0)),
                pl.BlockSpec((1, rows_v, V_DIM), per_b),
                pl.BlockSpec((1, LANES), const2)]
    in_specs += [pl.BlockSpec((1, 1) + kblk, kmap(i)) for i in range(n_pg)]
    in_specs += [pl.BlockSpec((1, 1, rows_v, V_DIM), vmap(i)) for i in range(n_pg)]
    grid_spec = pltpu.PrefetchScalarGridSpec(
        num_scalar_prefetch=1,
        grid=(db, n_pages // n_pg),
        in_specs=in_specs,
        out_specs=pl.BlockSpec((1, N_HEADS, V_DIM), per_b),
        scratch_shapes=[pltpu.VMEM((2 * N_HEADS, 1), F32),
                        pltpu.VMEM((2 * N_HEADS, 1), F32),
                        pltpu.VMEM((2 * N_HEADS, V_DIM), F32)])
    return pl.pallas_call(
        functools.partial(_decode_kernel, n_pg=n_pg, out_scale=out_scale),
        grid_spec=grid_spec,
        out_shape=jax.ShapeDtypeStruct((db, N_HEADS, V_DIM), F32),
        compiler_params=pltpu.CompilerParams(dimension_semantics=("parallel", "arbitrary"),
                                             vmem_limit_bytes=VMEM_LIMIT),
        name="attn_decode",
    )(pt_flat, lam, q16, bias, bias_new, k_new, v_new, sg, *([ck] * n_pg), *([cv] * n_pg))


def _split_bf16(x):
    hi = x.astype(BF16)
    return hi, (x - hi.astype(F32)).astype(BF16)


def _merge_kernel(mixed_ref, o_ref, ga_ref, gb_ref, x_ref, g1_ref, sc_ref, sh_ref, wa_ref, wb_ref,
                  wo_ref, n2g_ref, rwh_ref, rwl_ref, rb_ref, tri_ref,
                  xo_ref, h2_ref, idx_ref, gate_ref, rank_ref, cnt_ref, run_ref):
    @pl.when((pl.program_id(0) == 0) & (pl.program_id(1) == 0))
    def _():
        run_ref[...] = jnp.zeros(run_ref.shape, F32)

    a = jnp.dot(mixed_ref[0], wa_ref[...], preferred_element_type=F32)
    b = jnp.dot(o_ref[0], wb_ref[...], preferred_element_type=F32)
    merged = ga_ref[0].astype(F32) * a + gb_ref[0].astype(F32) * b
    xo = x_ref[0] + g1_ref[0] * jnp.dot(merged.astype(BF16), wo_ref[...], preferred_element_type=F32)
    xo_ref[0] = xo
    h2 = _rms(xo) * n2g_ref[...] * (1.0 + sc_ref[0]) + sh_ref[0]
    hi, lo = _split_bf16(h2)
    tm = hi.shape[0]
    hi32 = hi.astype(F32)
    for j in range(SUB):
        h2_ref[0, pl.ds(j, tm, stride=SUB), :] = hi32[:, j * LANES:(j + 1) * LANES]
    lg = (jnp.dot(hi, rwh_ref[...], preferred_element_type=F32)
          + jnp.dot(lo, rwh_ref[...], preferred_element_type=F32)
          + jnp.dot(hi, rwl_ref[...], preferred_element_type=F32)
          + rb_ref[...])

    lane = lax.broadcasted_iota(jnp.int32, lg.shape, 1)
    work = lg
    sels, vals, idxs = [], [], []
    for _ in range(TOP_K):
        mx = jnp.max(work, axis=-1, keepdims=True)
        ik = jnp.min(jnp.where(work == mx, lane, LANES), axis=-1, keepdims=True)
        sel = lane == ik
        work = jnp.where(sel, -jnp.inf, work)
        sels.append(sel)
        vals.append(mx)
        idxs.append(ik)
    es = [jnp.exp(v - vals[0]) for v in vals]
    den = es[0]
    for e in es[1:]:
        den = den + e

    onehot = sels[0].astype(F32)
    for sel in sels[1:]:
        onehot = onehot + sel.astype(F32)
    base = jnp.dot(tri_ref[...], onehot.astype(BF16), preferred_element_type=F32) + run_ref[...]
    idx_out = jnp.zeros(lg.shape, jnp.int32)
    gate_out = jnp.zeros(lg.shape, F32)
    rank_out = jnp.zeros(lg.shape, jnp.int32)
    for k in range(TOP_K):
        rk = jnp.sum(jnp.where(sels[k], base, 0.0), axis=-1, keepdims=True)
        idx_out = jnp.where(lane == k, idxs[k], idx_out)
        gate_out = jnp.where(lane == k, es[k] / den, gate_out)
        rank_out = jnp.where(lane == k, rk.astype(jnp.int32), rank_out)
    idx_ref[0] = idx_out
    gate_ref[0] = gate_out
    rank_ref[0] = rank_out
    run_ref[...] = run_ref[...] + jnp.sum(onehot, axis=0, keepdims=True)
    cnt_ref[...] = run_ref[...]


def _merge(mixed, o, ga, gb, x, g1, sc, sh, wa, wb, wo, n2g, rwh, rwl, rb, *, tm):
    nb, s, d = x.shape
    r = g1.shape[1]
    row = lambda b, i: (b, i, 0)
    mod = (lambda b, i: (b, i, 0)) if r == s else (lambda b, i: (b, 0, 0))
    rblk = tm if r == s else 1
    tok = pl.BlockSpec((1, tm, d), row)
    mods = pl.BlockSpec((1, rblk, d), mod)
    route = pl.BlockSpec((1, tm, LANES), row)
    tri = jnp.tril(jnp.ones((tm, tm), F32), -1).astype(BF16)
    return pl.pallas_call(
        _merge_kernel,
        grid=(nb, s // tm),
        in_specs=[tok, tok, tok, tok, tok, mods, mods, mods,
                  _const_spec((d, d)), _const_spec((d, d)), _const_spec((d, d)),
                  _const_spec((1, d)), _const_spec((d, LANES)), _const_spec((d, LANES)),
                  _const_spec((1, LANES)), _const_spec((tm, tm))],
        out_specs=[tok, pl.BlockSpec((1, tm * SUB, LANES), row), route, route, route,
                   pl.BlockSpec((1, LANES), lambda b, i: (0, 0))],
        out_shape=[jax.ShapeDtypeStruct((nb, s, d), F32),
                   jax.ShapeDtypeStruct((nb, s * SUB, LANES), F32),
                   jax.ShapeDtypeStruct((nb, s, LANES), jnp.int32),
                   jax.ShapeDtypeStruct((nb, s, LANES), F32),
                   jax.ShapeDtypeStruct((nb, s, LANES), jnp.int32),
                   jax.ShapeDtypeStruct((1, LANES), F32)],
        scratch_shapes=[pltpu.VMEM((1, LANES), F32)],
        compiler_params=pltpu.CompilerParams(dimension_semantics=("arbitrary", "arbitrary"),
                                             vmem_limit_bytes=VMEM_LIMIT),
        name="merge",
    )(mixed, o, ga, gb, x, g1, sc, sh, wa, wb, wo, n2g, rwh, rwl, rb, tri)


def _wprep_kernel(w1_ref, w2_ref, perm_ref, g_ref, l_ref, w2o_ref):
    for j in range(w1_ref.shape[-1] // MXU_DIM):
        z = jnp.dot(w1_ref[0, 0, :, j * MXU_DIM:(j + 1) * MXU_DIM].astype(BF16), perm_ref[...],
                    preferred_element_type=F32)
        g_ref[0, :, j * LANES:(j + 1) * LANES] = z[:, :LANES].astype(BF16)
        l_ref[0, :, j * LANES:(j + 1) * LANES] = z[:, LANES:].astype(BF16)
    w2o_ref[0] = w2_ref[0, 0].astype(BF16)


def _wprep(layer, w1, w2):
    _, n_e, d, f2 = w1.shape
    f = f2 // 2
    src = jnp.concatenate([jnp.arange(0, MXU_DIM, 2), jnp.arange(1, MXU_DIM, 2)])
    perm = (jnp.arange(MXU_DIM)[:, None] == src[None, :]).astype(BF16)
    return pl.pallas_call(
        _wprep_kernel,
        grid=(n_e,),
        in_specs=[pl.BlockSpec((1, 1, d, f2), lambda e: (layer, e, 0, 0)),
                  pl.BlockSpec((1, 1, f, d), lambda e: (layer, e, 0, 0)),
                  _const_spec((MXU_DIM, MXU_DIM))],
        out_specs=[pl.BlockSpec((1, d, f), lambda e: (e, 0, 0)),
                   pl.BlockSpec((1, d, f), lambda e: (e, 0, 0)),
                   pl.BlockSpec((1, f, d), lambda e: (e, 0, 0))],
        out_shape=[jax.ShapeDtypeStruct((n_e, d, f), BF16),
                   jax.ShapeDtypeStruct((n_e, d, f), BF16),
                   jax.ShapeDtypeStruct((n_e, f, d), BF16)],
        compiler_params=pltpu.CompilerParams(dimension_semantics=("parallel",),
                                             vmem_limit_bytes=VMEM_LIMIT),
        name="wprep",
    )(w1, w2, perm)


def _expert_kernel(be_ref, nu_ref, x_ref, w1g_ref, w1l_ref, b1g_ref, b1l_ref, w2_ref, b2_ref, y_ref):
    del be_ref
    tb = x_ref.shape[0] // SUB

    @pl.when(pl.program_id(0) < nu_ref[0])
    def _():
        x = jnp.concatenate([x_ref[pl.ds(j, tb, stride=SUB), :] for j in range(SUB)], axis=1).astype(BF16)
        glu = jnp.minimum(jnp.dot(x, w1g_ref[0], preferred_element_type=F32) + b1g_ref[0], SWIGLU_LIMIT)
        lin = jnp.clip(jnp.dot(x, w1l_ref[0], preferred_element_type=F32) + b1l_ref[0],
                       -SWIGLU_LIMIT, SWIGLU_LIMIT)
        act = glu * jax.nn.sigmoid(SWIGLU_ALPHA * glu) * (lin + 1.0)
        y = jnp.dot(act.astype(BF16), w2_ref[0], preferred_element_type=F32) + b2_ref[0]
        for j in range(SUB):
            y_ref[pl.ds(j, tb, stride=SUB), :] = y[:, j * LANES:(j + 1) * LANES]

    @pl.when(pl.program_id(0) >= nu_ref[0])
    def _():
        y_ref[...] = jnp.zeros(y_ref.shape, F32)


def _experts(block_expert, n_used, xs, w1g, w1l, b1g, b1l, w2, b2, *, tb):
    n_slots = xs.shape[0] // SUB
    n_e, d, f = w1g.shape
    emap = lambda i, be, nu: (be[i], 0, 0)
    grid_spec = pltpu.PrefetchScalarGridSpec(
        num_scalar_prefetch=2,
        grid=(n_slots // tb,),
        in_specs=[pl.BlockSpec((tb * SUB, LANES), lambda i, be, nu: (jnp.minimum(i, nu[0] - 1), 0)),
                  pl.BlockSpec((1, d, f), emap),
                  pl.BlockSpec((1, d, f), emap),
                  pl.BlockSpec((1, 1, f), emap),
                  pl.BlockSpec((1, 1, f), emap),
                  pl.BlockSpec((1, f, d), emap),
                  pl.BlockSpec((1, 1, d), emap)],
        out_specs=pl.BlockSpec((tb * SUB, LANES), lambda i, be, nu: (i, 0)))
    return pl.pallas_call(
        _expert_kernel,
        grid_spec=grid_spec,
        out_shape=jax.ShapeDtypeStruct((n_slots * SUB, LANES), F32),
        compiler_params=pltpu.CompilerParams(dimension_semantics=("arbitrary",),
                                             vmem_limit_bytes=VMEM_LIMIT),
        name="experts",
    )(block_expert, n_used, xs, w1g, w1l, b1g.reshape(n_e, 1, f), b1l.reshape(n_e, 1, f), w2,
      b2.reshape(n_e, 1, d))


def _dispatch_kernel(dest_ref, h_ref, xs_init_ref, xs_ref, sem):
    del xs_init_ref
    tm = h_ref.shape[0] // SUB

    def issue(t, carry):
        for k in range(TOP_K):
            slot = dest_ref[0, 0, t * TOP_K + k]
            pltpu.make_async_copy(h_ref.at[pl.ds(pl.multiple_of(t * SUB, SUB), SUB)],
                                  xs_ref.at[pl.ds(pl.multiple_of(slot * SUB, SUB), SUB)],
                                  sem).start(priority=k % 2)
        return carry

    lax.fori_loop(0, tm, issue, 0)
    for _ in range(TOP_K):
        pltpu.make_async_copy(h_ref, xs_ref.at[pl.ds(0, tm * SUB)], sem).wait()


def _dispatch(dest_tiles, h, n_slots, xs_init):
    if xs_init is None:
        xs_init = jnp.zeros((n_slots * SUB, LANES), F32)
    n_tiles, _, n = dest_tiles.shape
    tm = n // TOP_K
    return pl.pallas_call(
        _dispatch_kernel,
        grid=(n_tiles,),
        in_specs=[pl.BlockSpec((1, 1, n), lambda i: (i, 0, 0), memory_space=pltpu.SMEM),
                  pl.BlockSpec((tm * SUB, LANES), lambda i: (i, 0)),
                  pl.BlockSpec(memory_space=pl.ANY)],
        out_specs=pl.BlockSpec(memory_space=pl.ANY),
        out_shape=jax.ShapeDtypeStruct((n_slots * SUB, LANES), F32),
        scratch_shapes=[pltpu.SemaphoreType.DMA(())],
        input_output_aliases={2: 0},
        compiler_params=pltpu.CompilerParams(dimension_semantics=("arbitrary",)),
        name="dispatch",
    )(dest_tiles, h, xs_init)


def _combine_kernel(dcur_ref, dnext_ref, gate_ref, x_ref, g2_ref, yb_ref, o_ref, buf_ref, sem):
    i = pl.program_id(0)
    tm = x_ref.shape[0]

    def gather(d_ref, slot):
        def issue(t, carry):
            for k in range(TOP_K):
                src = pl.multiple_of(d_ref[0, 0, t * TOP_K + k] * SUB, SUB)
                pltpu.make_async_copy(yb_ref.at[pl.ds(src, SUB)],
                                      buf_ref.at[slot, k, pl.ds(pl.multiple_of(t * SUB, SUB), SUB)],
                                      sem.at[slot]).start(priority=k % 2)
            return carry
        lax.fori_loop(0, tm, issue, 0)

    @pl.when(i == 0)
    def _():
        gather(dcur_ref, 0)

    slot = i % 2

    @pl.when(i + 1 < pl.num_programs(0))
    def _():
        gather(dnext_ref, 1 - slot)

    for k in range(TOP_K):
        pltpu.make_async_copy(yb_ref.at[pl.ds(0, tm * SUB)], buf_ref.at[slot, k], sem.at[slot]).wait()
    gates = gate_ref[...]
    for j in range(SUB):
        cs = slice(j * LANES, (j + 1) * LANES)
        y = buf_ref[slot, 0, pl.ds(j, tm, stride=SUB), :] * gates[:, 0:1]
        for k in range(1, TOP_K):
            y = y + buf_ref[slot, k, pl.ds(j, tm, stride=SUB), :] * gates[:, k:k + 1]
        o_ref[:, cs] = x_ref[:, cs] + g2_ref[0, :, cs] * y


def _combine(dest_tiles, gates, x, g2, yb, *, tiles_per_seq):
    t, d = x.shape
    n_tiles, _, n = dest_tiles.shape
    tm = n // TOP_K
    if g2.shape[1] == 1:
        g2_spec = pl.BlockSpec((1, 1, d), lambda i: (i // tiles_per_seq, 0, 0))
    else:
        g2_spec = pl.BlockSpec((1, tm, d), lambda i: (i // tiles_per_seq, i % tiles_per_seq, 0))
    dspec = lambda f: pl.BlockSpec((1, 1, n), f, memory_space=pltpu.SMEM)
    return pl.pallas_call(
        _combine_kernel,
        grid=(n_tiles,),
        in_specs=[dspec(lambda i: (i, 0, 0)),
                  dspec(lambda i: (jnp.minimum(i + 1, n_tiles - 1), 0, 0)),
                  pl.BlockSpec((tm, LANES), lambda i: (i, 0)),
                  pl.BlockSpec((tm, d), lambda i: (i, 0)),
                  g2_spec,
                  pl.BlockSpec(memory_space=pl.ANY)],
        out_specs=pl.BlockSpec((tm, d), lambda i: (i, 0)),
        out_shape=jax.ShapeDtypeStruct((t, d), F32),
        scratch_shapes=[pltpu.VMEM((2, TOP_K, tm * SUB, LANES), F32),
                        pltpu.SemaphoreType.DMA((2,))],
        compiler_params=pltpu.CompilerParams(dimension_semantics=("arbitrary",),
                                             vmem_limit_bytes=VMEM_LIMIT),
        name="combine",
    )(dest_tiles, dest_tiles, gates, x, g2, yb)


def _moe(h2, idx, gates, rank, cnt, x, g2, w1g, w1l, b1g, b1l, w2, b2, *, tb, tm, xs_init=None):
    nb, s, d = x.shape
    t = nb * s
    n_assign = t * TOP_K
    counts = cnt[0, :N_EXPERTS].astype(jnp.int32)
    padded = (counts + tb - 1) // tb * tb
    pad_ends = jnp.cumsum(padded)
    pad_starts = pad_ends - padded
    idx4 = idx.reshape(t, LANES)[:, :TOP_K]
    rank4 = rank.reshape(t, LANES)[:, :TOP_K]
    experts = jnp.arange(N_EXPERTS, dtype=jnp.int32)
    dest = rank4 + jnp.sum(jnp.where(idx4[:, :, None] == experts, pad_starts, 0), axis=-1)
    dest_tiles = dest.reshape(t // tm, 1, tm * TOP_K)
    n_blocks = -(-n_assign // tb) + N_EXPERTS
    block_start = jnp.arange(n_blocks, dtype=jnp.int32) * tb
    block_expert = jnp.minimum(jnp.sum((block_start[:, None] >= pad_ends[None, :]).astype(jnp.int32), axis=1),
                               N_EXPERTS - 1)
    n_used = (pad_ends[-1:] // tb).astype(jnp.int32)
    xs = _dispatch(dest_tiles, h2.reshape(t * SUB, LANES), n_blocks * tb, xs_init)
    yb = _experts(block_expert, n_used, xs, w1g, w1l, b1g, b1l, w2, b2, tb=tb)
    out = _combine(dest_tiles, gates.reshape(t, LANES), x.reshape(t, d), g2, yb, tiles_per_seq=s // tm)
    return out.reshape(nb, s, d), yb


def _lambda_init(layer):
    return 0.8 - 0.6 * math.exp(-0.3 * layer)


def _decode_bias(n_pos, visible):
    col = jnp.arange(n_pos * KV_HEADS, dtype=jnp.int32)[None, :]
    row = jnp.arange(2 * N_HEADS, dtype=jnp.int32)[:, None]
    ok = ((col // n_pos) == ((row % N_HEADS) // GQA_GROUP)) & ((col % n_pos) < visible)
    return jnp.where(ok, 0.0, NEG).astype(F32)


def kernel(x_prompt, x_sample, c_prompt, c_sample, cache_k, cache_v, page_table, ada_w, ada_b, norm1_g, norm2_g, w_in, q_norm_g, k_norm_g, lambda_q1, lambda_k1, lambda_q2, lambda_k2, subln_g, ln_v_g, ln_v_b, w_spatial, b_spatial, w_branch_a, w_branch_b, w_out, router_w, router_b, w1, b1, w2, b2):
    depth = ada_w.shape[0]
    nb, seq, d = x_prompt.shape
    db, ds, _ = x_sample.shape
    assert ds == 1, "sampled tokens open a fresh chunk one row at a time"
    assert d == SUB * LANES, "MoE rows are moved as one float32 tile per token"
    n_phys, page = cache_k.shape[1], cache_k.shape[2]

    mods = _ada_mods(jnp.concatenate([c_prompt, c_sample], axis=0), ada_w, ada_b)
    ck = jnp.transpose(cache_k, (0, 1, 3, 4, 5, 2))
    cv = cache_v.reshape(depth, n_phys, page * KV_HEADS, V_DIM)
    bias = _decode_bias(page, page)
    bias_new = _decode_bias(page, ds)
    tri = jnp.tril(jnp.ones((CHUNK, CHUNK), F32))

    yp = x_prompt
    ys = x_sample.reshape(1, db, d)
    kp_l, vp_l, ks_l, vs_l, cv_l = [], [], [], [], []
    slots_p = slots_s = None
    for l in range(depth):
        lam_init = _lambda_init(l)
        lam = (jnp.exp(jnp.sum(lambda_q1[l] * lambda_k1[l])) - jnp.exp(jnp.sum(lambda_q2[l] * lambda_k2[l]))
               + lam_init).reshape(1).astype(F32)
        m6 = mods[l].reshape(nb + db, 6, d)
        mp = [m6[:nb, i][:, None, :] for i in range(6)]
        msm = [m6[nb:, i][None, :, :] for i in range(6)]
        w_in_b = w_in[l].astype(BF16)
        n1g = norm1_g[l][None, :]
        n2g = norm2_g[l][None, :]
        lng, lnb = ln_v_g[l][None, :], ln_v_b[l][None, :]
        qg = jnp.tile(q_norm_g[l], 2)[None, :]
        kg = jnp.tile(k_norm_g[l], 2)[None, :]
        sg = subln_g[l][None, :]
        sg_col = subln_g[l][:, None]
        ws_p = (w_spatial[l] * tri).astype(BF16)
        bs_p = jnp.repeat(b_spatial[l].T, LANES, axis=1)
        ws_s = jnp.repeat(w_spatial[l][:, 0, 0], LANES)[None, :]
        bs_s = jnp.repeat(b_spatial[l][:, 0], LANES)[None, :]
        wa, wb, wo = (w_branch_a[l].astype(BF16), w_branch_b[l].astype(BF16), w_out[l].astype(BF16))
        rw = jnp.pad(router_w[l], ((0, 0), (0, LANES - N_EXPERTS)))
        rwh, rwl = _split_bf16(rw)
        rb = jnp.pad(router_b[l], (0, LANES - N_EXPERTS), constant_values=NEG)[None, :]
        w1g, w1l, w2p = _wprep(l, w1, w2)
        b1g, b1l = b1[l][:, 0::2], b1[l][:, 1::2]
        ew = (w1g, w1l, b1g, b1l, w2p, b2[l])
        out_scale = 1.0 - lam_init

        mixed, q, k, v, kb, vb, ga, gb = _inproj(yp, mp[1], mp[0], n1g, w_in_b, lng, lnb, qg, kg,
                                                 ws_p, bs_p, sample=False, tm=TM_DENSE,
                                                 q_scale=HEAD_DIM ** -0.5 * LOG2E)
        o = _attn_prompt(lam, q, kb, vb, sg_col, out_scale=out_scale, tq=TQ, tk=TK)
        xo, h2, *route = _merge(mixed, o, ga, gb, yp, mp[2], mp[4], mp[3], wa, wb, wo, n2g, rwh, rwl, rb,
                                tm=TM_DENSE)
        yp, slots_p = _moe(h2, *route, xo, mp[5], *ew, tb=TB_PROMPT, tm=TM_DENSE, xs_init=slots_p)
        kp_l.append(k.reshape(nb, KV_HEADS, 2, HEAD_DIM, seq).transpose(0, 4, 1, 2, 3))
        vp_l.append(v.reshape(nb, seq, KV_HEADS, V_DIM))

        mixed, q, k, v, kb, vb, ga, gb, cvs = _inproj(ys, msm[1], msm[0], n1g, w_in_b, lng, lnb, qg, kg,
                                                      ws_s, bs_s, sample=True, tm=db,
                                                      q_scale=HEAD_DIM ** -0.5)
        q16 = q.reshape(db, N_HEADS, 2, HEAD_DIM).transpose(0, 2, 1, 3).reshape(db, 2 * N_HEADS, HEAD_DIM)
        k_new = jnp.pad(k.reshape(db, KV_HEADS, 2, HEAD_DIM, 1), ((0, 0),) * 4 + ((0, page - 1),))
        v_new = jnp.pad(v.reshape(db, KV_HEADS, V_DIM), ((0, 0), (0, (page - 1) * KV_HEADS), (0, 0)))
        o = _attn_decode(l, page_table, lam, q16, bias, bias_new, k_new, v_new, sg, ck, cv,
                         out_scale=out_scale)
        o = o.reshape(1, db, d).astype(BF16)
        xo, h2, *route = _merge(mixed, o, ga, gb, ys, msm[2], msm[4], msm[3], wa, wb, wo, n2g, rwh, rwl, rb,
                                tm=db)
        ys, slots_s = _moe(h2, *route, xo, msm[5], *ew, tb=TB_SAMPLE, tm=db, xs_init=slots_s)
        ks_l.append(k.reshape(db, ds, KV_HEADS, 2, HEAD_DIM))
        vs_l.append(v.reshape(db, ds, KV_HEADS, V_DIM))
        cv_l.append(cvs.reshape(db, ds, d))

    return (yp, ys.reshape(db, ds, d), jnp.stack(kp_l), jnp.stack(vp_l), jnp.stack(ks_l),
            jnp.stack(vs_l), jnp.stack(cv_l))
```
